```python
import math
import jax, jax.numpy as jnp
from jax import lax
import numpy as np

D_MODEL = 2048
BATCH = 4
SEQ = 2048
DEPTH = 4

N_MIXERS = 3
HEAD_DIM = 128
N_HEADS = D_MODEL // HEAD_DIM
ATTN_WIDTH = N_HEADS * HEAD_DIM
BLOCK_Q = 128
D_FF = 4 * D_MODEL
MLA_Q_RANK = 3 * D_MODEL // 8
MLA_KV_RANK = D_MODEL // 4
MLA_NOPE = 128
MLA_ROPE = 64
MLA_V = 128
MLA_QK = MLA_NOPE + MLA_ROPE
ROPE_THETA = 10000.0
EPS = 1e-6
N_SB = (DEPTH + 2) // 3
N_FOX = (DEPTH + 1) // 3
N_MLA = DEPTH // 3

kernel_name = "interleaved_sb_fox_mla_trunk"


def rmsnorm(x, g):
    xf = x.astype(jnp.float32)
    y = xf * lax.rsqrt(jnp.mean(xf * xf, axis=-1, keepdims=True) + EPS)
    return (y * g.astype(jnp.float32)).astype(x.dtype)


def sweep_query_blocks(block_fn, seq_len):
    outs = [block_fn(qs, qs + BLOCK_Q) for qs in range(0, seq_len, BLOCK_Q)]
    return jnp.concatenate(outs, axis=2)


def split_heads_qkv(qkv, b, s):
    t = qkv.reshape(b, s, 3, N_HEADS, HEAD_DIM).transpose(2, 0, 3, 1, 4)
    return t[0], t[1], t[2]


def merge_heads(o):
    b, h, s, d = o.shape
    return o.transpose(0, 2, 1, 3).reshape(b, s, h * d)


def stick_breaking_block(q_blk, k_pre, v_pre, q_start):
    n_q, n_k = q_blk.shape[2], k_pre.shape[2]
    z = jnp.einsum('bhtd,bhsd->bhts', q_blk, k_pre).astype(jnp.float32) * (1.0 / math.sqrt(HEAD_DIM))
    strict = jnp.arange(n_k)[None, :] < (q_start + jnp.arange(n_q))[:, None]
    log_keep = jnp.where(strict, jax.nn.log_sigmoid(-z), 0.0)
    later = lax.cumsum(log_keep, axis=3, reverse=True) - log_keep
    a = jnp.where(strict, jnp.exp(jax.nn.log_sigmoid(z) + later), 0.0)
    return jnp.einsum('bhts,bhsd->bhtd', a.astype(v_pre.dtype), v_pre)


def causal_softmax_block(q_blk, k_pre, v_pre, q_start, scale, bias=None):
    n_q, n_k = q_blk.shape[2], k_pre.shape[2]
    logits = jnp.einsum('bhtd,bhsd->bhts', q_blk, k_pre).astype(jnp.float32) * scale
    if bias is not None:
        logits = logits + bias
    causal = jnp.arange(n_k)[None, :] <= (q_start + jnp.arange(n_q))[:, None]
    p = jax.nn.softmax(jnp.where(causal, logits, -jnp.inf), axis=-1)
    return jnp.einsum('bhts,bhsd->bhtd', p.astype(v_pre.dtype), v_pre)


def stick_breaking_mixer(h, w_in, w_out):
    b, s, _ = h.shape
    q, k, v = split_heads_qkv(h @ w_in, b, s)
    o = sweep_query_blocks(
        lambda qs, qe: stick_breaking_block(q[:, :, qs:qe], k[:, :, :qe], v[:, :, :qe], qs), s)
    return merge_heads(o) @ w_out


def forgetting_mixer(h, w_in, b_f, q_gain, k_gain, w_out):
    b, s, _ = h.shape
    proj = h @ w_in
    q, k, v = split_heads_qkv(proj[..., :3 * ATTN_WIDTH], b, s)
    log_f = jax.nn.log_sigmoid((proj[..., 3 * ATTN_WIDTH:] + b_f).astype(jnp.float32))
    cf = jnp.cumsum(log_f, axis=1).transpose(0, 2, 1)
    q = rmsnorm(q, q_gain)
    k = rmsnorm(k, k_gain)
    scale = 1.0 / math.sqrt(HEAD_DIM)

    def blk(qs, qe):
        bias = cf[:, :, qs:qe, None] - cf[:, :, None, :qe]
        return causal_softmax_block(q[:, :, qs:qe], k[:, :, :qe], v[:, :, :qe], qs, scale, bias)

    return merge_heads(sweep_query_blocks(blk, s)) @ w_out


def rope(x, positions):
    half = x.shape[-1] // 2
    inv_freq = ROPE_THETA ** (-jnp.arange(0, half, dtype=jnp.float32) * 2.0 / x.shape[-1])
    ang = positions.astype(jnp.float32)[..., None] * inv_freq
    if x.ndim == 4:
        ang = ang[:, :, None, :]
    cos, sin = jnp.cos(ang), jnp.sin(ang)
    xf = x.astype(jnp.float32)
    x1, x2 = xf[..., :half], xf[..., half:]
    return jnp.concatenate([x1 * cos - x2 * sin, x1 * sin + x2 * cos], axis=-1).astype(x.dtype)


def mla_mixer(h, positions, w_in, q_norm, kv_norm, w_uq, w_ukv, q_gain, k_gain, w_out):
    b, s, _ = h.shape
    down = h @ w_in
    c_q = rmsnorm(down[..., :MLA_Q_RANK], q_norm)
    c_kv = rmsnorm(down[..., MLA_Q_RANK:MLA_Q_RANK + MLA_KV_RANK], kv_norm)
    k_rope = rope(down[..., MLA_Q_RANK + MLA_KV_RANK:], positions)
    q = (c_q @ w_uq).reshape(b, s, N_HEADS, MLA_QK)
    q = jnp.concatenate([q[..., :MLA_NOPE], rope(q[..., MLA_NOPE:], positions)], axis=-1)
    kv = (c_kv @ w_ukv).reshape(b, s, N_HEADS, MLA_NOPE + MLA_V)
    k = jnp.concatenate(
        [kv[..., :MLA_NOPE], jnp.broadcast_to(k_rope[:, :, None, :], (b, s, N_HEADS, MLA_ROPE))], axis=-1)
    v = kv[..., MLA_NOPE:].transpose(0, 2, 1, 3)
    q = rmsnorm(q, q_gain).transpose(0, 2, 1, 3)
    k = rmsnorm(k, k_gain).transpose(0, 2, 1, 3)
    scale = 1.0 / math.sqrt(MLA_QK)
    o = sweep_query_blocks(
        lambda qs, qe: causal_softmax_block(q[:, :, qs:qe], k[:, :, :qe], v[:, :, :qe], qs, scale), s)
    return merge_heads(o) @ w_out


def sq_relu_mlp(h, w1, w2):
    a = jax.nn.relu(h @ w1)
    return (a * a) @ w2


def setup_inputs(seed: int = 0) -> dict:
    key = jax.random.key(seed)
    ks = jax.random.split(key, 24)
    f32 = jnp.float32

    def nrm(k, shape, fan_in):
        return jax.random.normal(k, shape, f32) * (fan_in ** -0.5)

    def gain(k, shape):
        return 1.0 + 0.02 * jax.random.normal(k, shape, f32)

    return {
        "x": jax.random.normal(ks[0], (BATCH, SEQ, D_MODEL), f32),
        "positions": jnp.broadcast_to(jnp.arange(SEQ, dtype=jnp.int32)[None, :], (BATCH, SEQ)),
        "mix_norm": gain(ks[1], (DEPTH, D_MODEL)),
        "mlp_norm": gain(ks[2], (DEPTH, D_MODEL)),
        "sb_w_in": nrm(ks[3], (N_SB, D_MODEL, 3 * ATTN_WIDTH), D_MODEL),
        "sb_w_out": nrm(ks[4], (N_SB, ATTN_WIDTH, D_MODEL), ATTN_WIDTH),
        "fox_w_in": nrm(ks[5], (N_FOX, D_MODEL, 3 * ATTN_WIDTH + N_HEADS), D_MODEL),
        "fox_b_f": jax.random.uniform(ks[6], (N_FOX, N_HEADS), f32, 1.0, 5.0),
        "fox_q_gain": gain(ks[7], (N_FOX, HEAD_DIM)),
        "fox_k_gain": gain(ks[8], (N_FOX, HEAD_DIM)),
        "fox_w_out": nrm(ks[9], (N_FOX, ATTN_WIDTH, D_MODEL), ATTN_WIDTH),
        "mla_w_in": nrm(ks[10], (N_MLA, D_MODEL, MLA_Q_RANK + MLA_KV_RANK + MLA_ROPE), D_MODEL),
        "mla_q_norm": gain(ks[11], (N_MLA, MLA_Q_RANK)),
        "mla_kv_norm": gain(ks[12], (N_MLA, MLA_KV_RANK)),
        "mla_w_uq": nrm(ks[13], (N_MLA, MLA_Q_RANK, N_HEADS * MLA_QK), MLA_Q_RANK),
        "mla_w_ukv": nrm(ks[14], (N_MLA, MLA_KV_RANK, N_HEADS * (MLA_NOPE + MLA_V)), MLA_KV_RANK),
        "mla_q_gain": gain(ks[15], (N_MLA, MLA_QK)),
        "mla_k_gain": gain(ks[16], (N_MLA, MLA_QK)),
        "mla_w_out": nrm(ks[17], (N_MLA, N_HEADS * MLA_V, D_MODEL), N_HEADS * MLA_V),
        "mlp_w1": nrm(ks[18], (DEPTH, D_MODEL, D_FF), D_MODEL),
        "mlp_w2": nrm(ks[19], (DEPTH, D_FF, D_MODEL), D_FF),
    }


def reference(x, positions, mix_norm, mlp_norm, sb_w_in, sb_w_out, fox_w_in, fox_b_f, fox_q_gain,
              fox_k_gain, fox_w_out, mla_w_in, mla_q_norm, mla_kv_norm, mla_w_uq, mla_w_ukv,
              mla_q_gain, mla_k_gain, mla_w_out, mlp_w1, mlp_w2):
    for i in range(DEPTH):
        kind, j = i % N_MIXERS, i // N_MIXERS
        h = rmsnorm(x, mix_norm[i])
        if kind == 0:
            y = stick_breaking_mixer(h, sb_w_in[j], sb_w_out[j])
        elif kind == 1:
            y = forgetting_mixer(h, fox_w_in[j], fox_b_f[j], fox_q_gain[j], fox_k_gain[j], fox_w_out[j])
        else:
            y = mla_mixer(h, positions, mla_w_in[j], mla_q_norm[j], mla_kv_norm[j], mla_w_uq[j],
                          mla_w_ukv[j], mla_q_gain[j], mla_k_gain[j], mla_w_out[j])
        x = x + y
        x = x + sq_relu_mlp(rmsnorm(x, mlp_norm[i]), mlp_w1[i], mlp_w2[i])
    return x
```

```python
import functools
import math

import jax
import jax.numpy as jnp
from jax import lax
from jax.experimental import pallas as pl
from jax.experimental.pallas import tpu as pltpu

F32 = jnp.float32
BF16 = jnp.bfloat16

D_MODEL = 2048
HEAD_DIM = 128
N_HEADS = D_MODEL // HEAD_DIM
ATTN_WIDTH = N_HEADS * HEAD_DIM
D_FF = 4 * D_MODEL
MLA_Q_RANK = 3 * D_MODEL // 8
MLA_KV_RANK = D_MODEL // 4
MLA_NOPE = 128
MLA_ROPE = 64
MLA_V = 128
MLA_QK = MLA_NOPE + MLA_ROPE
MLA_QK_PAD = 256
ROPE_THETA = 10000.0
EPS = 1e-6
N_MIXERS = 3

LANES = 128
ATTN_TQ = 256
ATTN_TK = 256
PREP_ROWS = 256
VMEM_LIMIT = 48 * 1024 * 1024

_NT = (((1,), (1,)), ((), ()))


def _params(n_grid):
    return pltpu.CompilerParams(dimension_semantics=("arbitrary",) * n_grid,
                                vmem_limit_bytes=VMEM_LIMIT)


def _rms(x):
    ms = jnp.mean(x * x, axis=-1, keepdims=True)
    return x * lax.rsqrt(ms + EPS)


def _rope(x, cos2, sin2):
    half = x.shape[-1] // 2
    rot = jnp.concatenate([-x[:, half:], x[:, :half]], axis=1)
    return x * cos2 + rot * sin2


def _mm_kernel(*refs, norm, nt, epi, n_extra, has_res):
    it = iter(refs)
    x_ref = next(it)
    g_ref = next(it) if norm else None
    w_ref = next(it)
    extra = [next(it) for _ in range(n_extra)]
    res_ref = next(it) if has_res else None
    o_ref = next(it)
    h_ref = next(it) if norm else None

    if norm:
        @pl.when(pl.program_id(2) == 0)
        def _():
            h_ref[...] = (_rms(x_ref[0]) * g_ref[...]).astype(BF16)
        h = h_ref[...]
    else:
        h = x_ref[0]
    if nt:
        acc = lax.dot_general(w_ref[...], h, _NT, preferred_element_type=F32)
    else:
        acc = jnp.dot(h, w_ref[...], preferred_element_type=F32)
    if epi is not None:
        acc = epi(acc, *extra)
    if has_res:
        acc = res_ref[0] + acc
    o_ref[0] = acc.astype(o_ref.dtype)


def _matmul(x, w, *, gain=None, nt=False, epi=None, extras=(), res=None, out_dtype, tm, tn, name):
    B, S, K = x.shape
    N = w.shape[0] if nt else w.shape[1]
    norm = gain is not None
    grid = (B, S // tm, N // tn)
    in_specs = [pl.BlockSpec((1, tm, K), lambda b, i, j: (b, i, 0))]
    args = [x]
    if norm:
        in_specs.append(pl.BlockSpec((1, K), lambda b, i, j: (0, 0)))
        args.append(gain)
    if nt:
        in_specs.append(pl.BlockSpec((tn, K), lambda b, i, j: (j, 0)))
    else:
        in_specs.append(pl.BlockSpec((K, tn), lambda b, i, j: (0, j)))
    args.append(w)
    for e in extras:
        in_specs.append(pl.BlockSpec((1, tn), lambda b, i, j: (0, j)))
        args.append(e)
    if res is not None:
        in_specs.append(pl.BlockSpec((1, tm, tn), lambda b, i, j: (b, i, j)))
        args.append(res)
    if nt:
        out_shape = jax.ShapeDtypeStruct((B, N, S), out_dtype)
        out_spec = pl.BlockSpec((1, tn, tm), lambda b, i, j: (b, j, i))
    else:
        out_shape = jax.ShapeDtypeStruct((B, S, N), out_dtype)
        out_spec = pl.BlockSpec((1, tm, tn), lambda b, i, j: (b, i, j))
    body = functools.partial(_mm_kernel, norm=norm, nt=nt, epi=epi, n_extra=len(extras),
                             has_res=res is not None)
    return pl.pallas_call(
        body, grid=grid, in_specs=in_specs, out_specs=out_spec, out_shape=out_shape,
        scratch_shapes=[pltpu.VMEM((tm, K), BF16)] if norm else [],
        compiler_params=_params(3), name=name)(*args)


def _relu2_epi(acc):
    a = jnp.maximum(acc, 0.0)
    return a * a


def _head_norm_epi(acc, g_ref):
    g = g_ref[...]
    outs = []
    for c in range(acc.shape[1] // HEAD_DIM):
        sl = slice(c * HEAD_DIM, (c + 1) * HEAD_DIM)
        outs.append(_rms(acc[:, sl]) * g[:, sl])
    return jnp.concatenate(outs, axis=1)


def _softmax_sweep(q_ref, k_ref, vT_ref, o_ref, acc_ref, m_ref, l_ref, *, scale, bias, S):
    tq, tk = ATTN_TQ, ATTN_TK

    def q_body(qi, carry):
        q0 = pl.multiple_of(qi * tq, tq)
        q = q_ref[pl.ds(q0, tq), :]
        m_ref[...] = jnp.full(m_ref.shape, -jnp.inf, F32)
        l_ref[...] = jnp.zeros(l_ref.shape, F32)
        acc_ref[...] = jnp.zeros(acc_ref.shape, F32)

        def tile(k0, masked):
            k = k_ref[pl.ds(k0, tk), :]
            s = lax.dot_general(k, q, _NT, preferred_element_type=F32) * scale
            b = bias(k0, q0)
            if b is not None:
                s = s + b
            if masked:
                key = lax.broadcasted_iota(jnp.int32, (tk, tq), 0)
                qry = lax.broadcasted_iota(jnp.int32, (tk, tq), 1)
                s = jnp.where(key <= qry, s, -jnp.inf)
            m_prev = m_ref[...]
            m_new = jnp.maximum(m_prev, jnp.max(s, axis=0, keepdims=True))
            alpha = jnp.exp(m_prev - m_new)
            p = jnp.exp(s - m_new)
            l_ref[...] = alpha * l_ref[...] + jnp.sum(p, axis=0, keepdims=True)
            pv = jnp.dot(vT_ref[0, :, pl.ds(k0, tk)], p.astype(BF16), preferred_element_type=F32)
            acc_ref[...] = alpha * acc_ref[...] + pv
            m_ref[...] = m_new

        def k_body(ki, c):
            tile(pl.multiple_of(ki * tk, tk), False)
            return c

        lax.fori_loop(0, qi, k_body, 0)
        tile(q0, True)
        o = acc_ref[...] / l_ref[...]
        o_ref[0, pl.ds(q0, tq), :] = o.T.astype(o_ref.dtype)
        return carry

    lax.fori_loop(0, S // tq, q_body, 0)


def _sb_attn_kernel(q_ref, k_ref, vT_ref, u_ref, o_ref, acc_ref, suf_ref, *, scale, S):
    tq, tk = ATTN_TQ, ATTN_TK

    def q_body(qi, carry):
        q0 = pl.multiple_of(qi * tq, tq)
        q = q_ref[0, pl.ds(q0, tq), :]
        suf_ref[...] = jnp.zeros(suf_ref.shape, F32)
        acc_ref[...] = jnp.zeros(acc_ref.shape, F32)

        def tile(k0, masked):
            k = k_ref[0, pl.ds(k0, tk), :]
            z = lax.dot_general(k, q, _NT, preferred_element_type=F32) * scale
            soft = jnp.log1p(jnp.exp(-jnp.abs(z)))
            log_beta = jnp.minimum(z, 0.0) - soft
            log_keep = jnp.minimum(-z, 0.0) - soft
            if masked:
                key = lax.broadcasted_iota(jnp.int32, (tk, tq), 0)
                qry = lax.broadcasted_iota(jnp.int32, (tk, tq), 1)
                strict = key < qry
                log_keep = jnp.where(strict, log_keep, 0.0)
            hi = log_keep.astype(BF16)
            lo = (log_keep - hi.astype(F32)).astype(BF16)
            later = (jnp.dot(u_ref[...], hi, preferred_element_type=F32)
                     + jnp.dot(u_ref[...], lo, preferred_element_type=F32))
            a = jnp.exp(log_beta + (later + suf_ref[...]))
            if masked:
                a = jnp.where(strict, a, 0.0)
            acc_ref[...] += jnp.dot(vT_ref[0, :, pl.ds(k0, tk)], a.astype(BF16),
                                    preferred_element_type=F32)
            suf_ref[...] += jnp.sum(log_keep, axis=0, keepdims=True)

        tile(q0, True)

        def k_body(i, c):
            tile(pl.multiple_of((qi - 1 - i) * tk, tk), False)
            return c

        lax.fori_loop(0, qi, k_body, 0)
        o_ref[0, pl.ds(q0, tq), :] = acc_ref[...].T.astype(o_ref.dtype)
        return carry

    lax.fori_loop(0, S // tq, q_body, 0)


def _sb_attention(qk, vT, B, S):
    tk = ATTN_TK
    upper = (jnp.arange(tk)[None, :] > jnp.arange(tk)[:, None]).astype(BF16)
    H = N_HEADS
    return pl.pallas_call(
        functools.partial(_sb_attn_kernel, scale=1.0 / math.sqrt(HEAD_DIM), S=S),
        grid=(B, H),
        in_specs=[pl.BlockSpec((1, S, HEAD_DIM), lambda b, h: (b, 0, h)),
                  pl.BlockSpec((1, S, HEAD_DIM), lambda b, h: (b, 0, H + h)),
                  pl.BlockSpec((1, HEAD_DIM, S), lambda b, h: (b, h, 0)),
                  pl.BlockSpec((tk, tk), lambda b, h: (0, 0))],
        out_specs=pl.BlockSpec((1, S, HEAD_DIM), lambda b, h: (b, 0, h)),
        out_shape=jax.ShapeDtypeStruct((B, S, ATTN_WIDTH), BF16),
        scratch_shapes=[pltpu.VMEM((HEAD_DIM, ATTN_TQ), F32), pltpu.VMEM((1, ATTN_TQ), F32)],
        compiler_params=_params(2), name="sb_attention")(qk, qk, vT, upper)


def _fox_attn_kernel(q_ref, k_ref, vT_ref, cfc_ref, cfr_ref, o_ref, slab_ref, acc_ref, m_ref, l_ref,
                     *, scale, S):
    h = pl.program_id(1)

    def prep(c, carry):
        r0 = pl.multiple_of(c * PREP_ROWS, PREP_ROWS)
        blk = cfc_ref[0, pl.ds(r0, PREP_ROWS), :]
        lane = lax.broadcasted_iota(jnp.int32, blk.shape, 1)
        col = jnp.sum(jnp.where(lane == h, blk, 0.0), axis=1, keepdims=True)
        slab_ref[pl.ds(r0, PREP_ROWS), :] = jnp.broadcast_to(col, blk.shape)
        return carry

    lax.fori_loop(0, S // PREP_ROWS, prep, 0)

    def bias(k0, q0):
        cf_q = cfr_ref[0, 0, :, pl.ds(q0, ATTN_TQ)]
        cf_k = slab_ref[pl.ds(k0, ATTN_TK), :]
        return cf_q - jnp.concatenate([cf_k] * (ATTN_TQ // LANES), axis=1)

    _softmax_sweep(q_ref.at[0], k_ref.at[0], vT_ref, o_ref, acc_ref, m_ref, l_ref,
                   scale=scale, bias=bias, S=S)


def _fox_attention(qk, vT, cf_col, cf_row, B, S):
    H = N_HEADS
    return pl.pallas_call(
        functools.partial(_fox_attn_kernel, scale=1.0 / math.sqrt(HEAD_DIM), S=S),
        grid=(B, H),
        in_specs=[pl.BlockSpec((1, S, HEAD_DIM), lambda b, h: (b, 0, h)),
                  pl.BlockSpec((1, S, HEAD_DIM), lambda b, h: (b, 0, H + h)),
                  pl.BlockSpec((1, HEAD_DIM, S), lambda b, h: (b, h, 0)),
                  pl.BlockSpec((1, S, LANES), lambda b, h: (b, 0, 0)),
                  pl.BlockSpec((1, 1, 1, S), lambda b, h: (b, h, 0, 0))],
        out_specs=pl.BlockSpec((1, S, HEAD_DIM), lambda b, h: (b, 0, h)),
        out_shape=jax.ShapeDtypeStruct((B, S, ATTN_WIDTH), BF16),
        scratch_shapes=[pltpu.VMEM((S, LANES), F32),
                        pltpu.VMEM((HEAD_DIM, ATTN_TQ), F32),
                        pltpu.VMEM((1, ATTN_TQ), F32), pltpu.VMEM((1, ATTN_TQ), F32)],
        compiler_params=_params(2), name="fox_attention")(qk, qk, vT, cf_col, cf_row)


def _mla_attn_kernel(qf_ref, kn_ref, kr_ref, cos_ref, sin_ref, qg_ref, kg_ref, vT_ref, o_ref,
                     q_scr, k_scr, acc_ref, m_ref, l_ref, *, scale, S):
    qg = qg_ref[...]
    kg = kg_ref[...]
    pad = jnp.zeros((PREP_ROWS, MLA_QK_PAD - MLA_QK), F32)

    def normed(nope, roped, g):
        ssq = (jnp.sum(nope * nope, axis=-1, keepdims=True)
               + jnp.sum(roped * roped, axis=-1, keepdims=True))
        rs = lax.rsqrt(ssq * (1.0 / MLA_QK) + EPS)
        full = jnp.concatenate([nope * rs * g[:, :MLA_NOPE], roped * rs * g[:, MLA_NOPE:MLA_QK], pad],
                               axis=1)
        return full.astype(BF16)

    def prep(c, carry):
        r0 = pl.multiple_of(c * PREP_ROWS, PREP_ROWS)
        rows = pl.ds(r0, PREP_ROWS)
        qf = qf_ref[0, rows, :]
        q_rope = _rope(qf[:, MLA_NOPE:MLA_QK], cos_ref[0, rows, :], sin_ref[0, rows, :])
        q_scr[rows, :] = normed(qf[:, :MLA_NOPE], q_rope, qg)
        k_scr[rows, :] = normed(kn_ref[0, rows, :], kr_ref[0, rows, :], kg)
        return carry

    lax.fori_loop(0, S // PREP_ROWS, prep, 0)

    _softmax_sweep(q_scr, k_scr, vT_ref, o_ref, acc_ref, m_ref, l_ref,
                   scale=scale, bias=lambda k0, q0: None, S=S)


def _mla_attention(q_full, k_nope, k_rope, cos2, sin2, q_gain, k_gain, vT, B, S):
    return pl.pallas_call(
        functools.partial(_mla_attn_kernel, scale=1.0 / math.sqrt(MLA_QK), S=S),
        grid=(B, N_HEADS),
        in_specs=[pl.BlockSpec((1, S, MLA_QK_PAD), lambda b, h: (b, 0, h)),
                  pl.BlockSpec((1, S, MLA_NOPE), lambda b, h: (b, 0, h)),
                  pl.BlockSpec((1, S, MLA_ROPE), lambda b, h: (b, 0, 0)),
                  pl.BlockSpec((1, S, MLA_ROPE), lambda b, h: (b, 0, 0)),
                  pl.BlockSpec((1, S, MLA_ROPE), lambda b, h: (b, 0, 0)),
                  pl.BlockSpec((1, MLA_QK_PAD), lambda b, h: (0, 0)),
                  pl.BlockSpec((1, MLA_QK_PAD), lambda b, h: (0, 0)),
                  pl.BlockSpec((1, MLA_V, S), lambda b, h: (b, h, 0))],
        out_specs=pl.BlockSpec((1, S, MLA_V), lambda b, h: (b, 0, h)),
        out_shape=jax.ShapeDtypeStruct((B, S, N_HEADS * MLA_V), BF16),
        scratch_shapes=[pltpu.VMEM((S, MLA_QK_PAD), BF16), pltpu.VMEM((S, MLA_QK_PAD), BF16),
                        pltpu.VMEM((MLA_V, ATTN_TQ), F32),
                        pltpu.VMEM((1, ATTN_TQ), F32), pltpu.VMEM((1, ATTN_TQ), F32)],
        compiler_params=_params(2), name="mla_attention")(
            q_full, k_nope, k_rope, cos2, sin2, q_gain, k_gain, vT)


def _fox_gate_kernel(x_ref, g_ref, w_ref, b_ref, tri_ref, col_ref, row_ref, carry_ref):
    @pl.when(pl.program_id(1) == 0)
    def _():
        carry_ref[...] = jnp.zeros(carry_ref.shape, F32)

    h = (_rms(x_ref[0]) * g_ref[...]).astype(BF16)
    f = jnp.dot(h, w_ref[...], preferred_element_type=F32) + b_ref[...]
    log_f = jnp.minimum(f, 0.0) - jnp.log1p(jnp.exp(-jnp.abs(f)))
    p0 = log_f.astype(BF16)
    r1 = log_f - p0.astype(F32)
    p1 = r1.astype(BF16)
    p2 = (r1 - p1.astype(F32)).astype(BF16)
    tri = tri_ref[...]
    cs = (jnp.dot(tri, p0, preferred_element_type=F32) + jnp.dot(tri, p1, preferred_element_type=F32)
          + jnp.dot(tri, p2, preferred_element_type=F32)) + carry_ref[...]
    col_ref[0] = cs
    row_ref[0] = cs.T
    carry_ref[...] = cs[cs.shape[0] - 1:, :]


def _fox_gate(x, gain, w_f, b_f, ts=512):
    B, S, K = x.shape
    tri = (jnp.arange(ts)[None, :] <= jnp.arange(ts)[:, None]).astype(BF16)
    return pl.pallas_call(
        _fox_gate_kernel, grid=(B, S // ts),
        in_specs=[pl.BlockSpec((1, ts, K), lambda b, i: (b, i, 0)),
                  pl.BlockSpec((1, K), lambda b, i: (0, 0)),
                  pl.BlockSpec((K, LANES), lambda b, i: (0, 0)),
                  pl.BlockSpec((1, LANES), lambda b, i: (0, 0)),
                  pl.BlockSpec((ts, ts), lambda b, i: (0, 0))],
        out_specs=[pl.BlockSpec((1, ts, LANES), lambda b, i: (b, i, 0)),
                   pl.BlockSpec((1, LANES, ts), lambda b, i: (b, 0, i))],
        out_shape=[jax.ShapeDtypeStruct((B, S, LANES), F32), jax.ShapeDtypeStruct((B, LANES, S), F32)],
        scratch_shapes=[pltpu.VMEM((1, LANES), F32)],
        compiler_params=_params(2), name="fox_gate")(x, gain, w_f, b_f, tri)


def _rope_table_kernel(pos_ref, inv_ref, cos_ref, sin_ref):
    ang = pos_ref[0].astype(F32) * inv_ref[...]
    cos_ref[0] = jnp.cos(ang)
    sin_ref[0] = jnp.sin(ang)


def _rope_tables(positions):
    B, S = positions.shape
    half = MLA_ROPE // 2
    inv_freq = ROPE_THETA ** (-jnp.arange(0, half, dtype=F32) * 2.0 / MLA_ROPE)
    inv2 = jnp.concatenate([inv_freq, inv_freq])[None, :]
    out = jax.ShapeDtypeStruct((B, S, MLA_ROPE), F32)
    return pl.pallas_call(
        _rope_table_kernel, grid=(B,),
        in_specs=[pl.BlockSpec((1, S, 1), lambda b: (b, 0, 0)),
                  pl.BlockSpec((1, MLA_ROPE), lambda b: (0, 0))],
        out_specs=[pl.BlockSpec((1, S, MLA_ROPE), lambda b: (b, 0, 0))] * 2,
        out_shape=[out, out], compiler_params=_params(1), name="rope_tables")(
            positions.reshape(B, S, 1), inv2)


def _mla_down_kernel(x_ref, g_ref, w_ref, qn_ref, kvn_ref, cos_ref, sin_ref, cq_ref, ckv_ref, kr_ref):
    h = (_rms(x_ref[0]) * g_ref[...]).astype(BF16)
    down = jnp.dot(h, w_ref[...], preferred_element_type=F32)
    kv0 = MLA_Q_RANK
    r0 = MLA_Q_RANK + MLA_KV_RANK
    cq_ref[0] = (_rms(down[:, :kv0]) * qn_ref[...]).astype(BF16)
    ckv_ref[0] = (_rms(down[:, kv0:r0]) * kvn_ref[...]).astype(BF16)
    kr_ref[0] = _rope(down[:, r0:r0 + MLA_ROPE], cos_ref[0], sin_ref[0])


def _mla_down(x, gain, w_pad, q_norm, kv_norm, cos2, sin2, tm=512):
    B, S, K = x.shape
    n_pad = w_pad.shape[1]
    row = lambda b, i: (b, i, 0)
    fixed = lambda b, i: (0, 0)
    return pl.pallas_call(
        _mla_down_kernel, grid=(B, S // tm),
        in_specs=[pl.BlockSpec((1, tm, K), row), pl.BlockSpec((1, K), fixed),
                  pl.BlockSpec((K, n_pad), fixed),
                  pl.BlockSpec((1, MLA_Q_RANK), fixed), pl.BlockSpec((1, MLA_KV_RANK), fixed),
                  pl.BlockSpec((1, tm, MLA_ROPE), row), pl.BlockSpec((1, tm, MLA_ROPE), row)],
        out_specs=[pl.BlockSpec((1, tm, MLA_Q_RANK), row), pl.BlockSpec((1, tm, MLA_KV_RANK), row),
                   pl.BlockSpec((1, tm, MLA_ROPE), row)],
        out_shape=[jax.ShapeDtypeStruct((B, S, MLA_Q_RANK), BF16),
                   jax.ShapeDtypeStruct((B, S, MLA_KV_RANK), BF16),
                   jax.ShapeDtypeStruct((B, S, MLA_ROPE), F32)],
        compiler_params=_params(2), name="mla_down")(x, gain, w_pad, q_norm, kv_norm, cos2, sin2)


def _sb_layer(x, gain, w_in, w_out):
    B, S, _ = x.shape
    w_qk = w_in[:, :2 * ATTN_WIDTH].astype(BF16)
    w_vT = w_in[:, 2 * ATTN_WIDTH:].T.astype(BF16)
    qk = _matmul(x, w_qk, gain=gain, out_dtype=BF16, tm=512, tn=1024, name="sb_qk")
    vT = _matmul(x, w_vT, gain=gain, nt=True, out_dtype=BF16, tm=512, tn=512, name="sb_vT")
    o = _sb_attention(qk, vT, B, S)
    return _matmul(o, w_out.astype(BF16), res=x, out_dtype=F32, tm=512, tn=1024, name="sb_out")


def _fox_layer(x, gain, w_in, b_f, q_gain, k_gain, w_out):
    B, S, _ = x.shape
    w_qk = w_in[:, :2 * ATTN_WIDTH].astype(BF16)
    w_vT = w_in[:, 2 * ATTN_WIDTH:3 * ATTN_WIDTH].T.astype(BF16)
    w_f = jnp.pad(w_in[:, 3 * ATTN_WIDTH:], ((0, 0), (0, LANES - N_HEADS))).astype(BF16)
    b_pad = jnp.pad(b_f, (0, LANES - N_HEADS))[None, :]
    qk_gain = jnp.concatenate([jnp.tile(q_gain, N_HEADS), jnp.tile(k_gain, N_HEADS)])[None, :]
    qk = _matmul(x, w_qk, gain=gain, epi=_head_norm_epi, extras=(qk_gain,), out_dtype=BF16,
                 tm=512, tn=1024, name="fox_qk")
    vT = _matmul(x, w_vT, gain=gain, nt=True, out_dtype=BF16, tm=512, tn=512, name="fox_vT")
    cf_col, cf_rowT = _fox_gate(x, gain, w_f, b_pad)
    cf_row = cf_rowT[:, :N_HEADS, :].reshape(B, N_HEADS, 1, S)
    o = _fox_attention(qk, vT, cf_col, cf_row, B, S)
    return _matmul(o, w_out.astype(BF16), res=x, out_dtype=F32, tm=512, tn=1024, name="fox_out")


def _mla_layer(x, positions, gain, w_in, q_norm, kv_norm, w_uq, w_ukv, q_gain, k_gain, w_out):
    B, S, _ = x.shape
    n_down = w_in.shape[1]
    n_pad = -(-n_down // LANES) * LANES
    w_down = jnp.pad(w_in, ((0, 0), (0, n_pad - n_down))).astype(BF16)
    w_q = jnp.pad(w_uq.reshape(MLA_Q_RANK, N_HEADS, MLA_QK),
                  ((0, 0), (0, 0), (0, MLA_QK_PAD - MLA_QK))).reshape(MLA_Q_RANK, -1).astype(BF16)
    w_kv = w_ukv.reshape(MLA_KV_RANK, N_HEADS, MLA_NOPE + MLA_V)
    w_k = w_kv[:, :, :MLA_NOPE].reshape(MLA_KV_RANK, -1).astype(BF16)
    w_vT = w_kv[:, :, MLA_NOPE:].reshape(MLA_KV_RANK, -1).T.astype(BF16)
    qg = jnp.pad(q_gain, (0, MLA_QK_PAD - MLA_QK))[None, :]
    kg = jnp.pad(k_gain, (0, MLA_QK_PAD - MLA_QK))[None, :]

    cos2, sin2 = _rope_tables(positions)
    c_q, c_kv, k_rope = _mla_down(x, gain, w_down, q_norm[None, :], kv_norm[None, :], cos2, sin2)
    q_full = _matmul(c_q, w_q, out_dtype=F32, tm=512, tn=1024, name="mla_uq")
    k_nope = _matmul(c_kv, w_k, out_dtype=F32, tm=512, tn=1024, name="mla_uk")
    vT = _matmul(c_kv, w_vT, nt=True, out_dtype=BF16, tm=512, tn=512, name="mla_uvT")
    o = _mla_attention(q_full, k_nope, k_rope, cos2, sin2, qg, kg, vT, B, S)
    return _matmul(o, w_out.astype(BF16), res=x, out_dtype=F32, tm=512, tn=1024, name="mla_out")


def _mlp(x, gain, w1, w2):
    a = _matmul(x, w1.astype(BF16), gain=gain, epi=_relu2_epi, out_dtype=BF16, tm=512, tn=1024,
                name="mlp_up")
    return _matmul(a, w2.astype(BF16), res=x, out_dtype=F32, tm=512, tn=512, name="mlp_down")


def kernel(x, positions, mix_norm, mlp_norm, sb_w_in, sb_w_out, fox_w_in, fox_b_f, fox_q_gain,
           fox_k_gain, fox_w_out, mla_w_in, mla_q_norm, mla_kv_norm, mla_w_uq, mla_w_ukv,
           mla_q_gain, mla_k_gain, mla_w_out, mlp_w1, mlp_w2):
    depth = mix_norm.shape[0]
    for i in range(depth):
        kind, j = i % N_MIXERS, i // N_MIXERS
        gain = mix_norm[i][None, :]
        if kind == 0:
            x = _sb_layer(x, gain, sb_w_in[j], sb_w_out[j])
        elif kind == 1:
            x = _fox_layer(x, gain, fox_w_in[j], fox_b_f[j], fox_q_gain[j], fox_k_gain[j], fox_w_out[j])
        else:
            x = _mla_layer(x, positions, gain, mla_w_in[j], mla_q_norm[j], mla_kv_norm[j], mla_w_uq[j],
                           mla_w_ukv[j], mla_q_gain[j], mla_k_gain[j], mla_w_out[j])
        x = _mlp(x, mlp_norm[i][None, :], mlp_w1[i], mlp_w2[i])
    return x
```

```python
import functools
import math

import jax
import jax.numpy as jnp
from jax import lax
from jax.experimental import pallas as pl
from jax.experimental.pallas import tpu as pltpu

F32 = jnp.float32
BF16 = jnp.bfloat16

D_MODEL = 2048
HEAD_DIM = 128
N_HEADS = D_MODEL // HEAD_DIM
ATTN_WIDTH = N_HEADS * HEAD_DIM
D_FF = 4 * D_MODEL
MLA_Q_RANK = 3 * D_MODEL // 8
MLA_KV_RANK = D_MODEL // 4
MLA_NOPE = 128
MLA_ROPE = 64
MLA_V = 128
MLA_QK = MLA_NOPE + MLA_ROPE
MLA_QK_PAD = 256
ROPE_THETA = 10000.0
EPS = 1e-6
N_MIXERS = 3
LOG2E = math.log2(math.e)

LANES = 128
ATTN_TQ = 256
ATTN_TK = 256
PREP_ROWS = 256
SCORE_LOOKAHEAD = 5
SCAN_LOOKAHEAD = 2
VMEM_LIMIT = 48 * 1024 * 1024

_NT = (((1,), (1,)), ((), ()))


def _params(n_grid):
    return pltpu.CompilerParams(dimension_semantics=("arbitrary",) * n_grid,
                                vmem_limit_bytes=VMEM_LIMIT)


def _rms(x):
    ms = jnp.mean(x * x, axis=-1, keepdims=True)
    return x * lax.rsqrt(ms + EPS)


def _rope(x, cos2, sin2):
    half = x.shape[-1] // 2
    rot = jnp.concatenate([-x[:, half:], x[:, :half]], axis=1)
    return x * cos2 + rot * sin2


def _mm_kernel(*refs, norm, epi, epi_tiles, n_extra, has_res):
    it = iter(refs)
    x_ref = next(it)
    g_ref = next(it) if norm else None
    w_ref = next(it)
    extra = [next(it) for _ in range(n_extra)]
    res_ref = next(it) if has_res else None
    o_ref = next(it)
    h_ref = next(it) if norm else None

    if norm:
        @pl.when(pl.program_id(2) == 0)
        def _():
            h_ref[...] = (_rms(x_ref[0]) * g_ref[...]).astype(BF16)
        h = h_ref[...]
    else:
        h = x_ref[0]
    acc = jnp.dot(h, w_ref[...], preferred_element_type=F32)
    if has_res:
        acc = res_ref[0] + acc
    if epi is None:
        o_ref[0] = acc.astype(o_ref.dtype)
    elif epi_tiles is None:
        o_ref[0] = epi(acc, *extra).astype(o_ref.dtype)
    else:
        @pl.when(pl.program_id(2) < epi_tiles)
        def _():
            o_ref[0] = epi(acc, *extra).astype(o_ref.dtype)

        @pl.when(pl.program_id(2) >= epi_tiles)
        def _():
            o_ref[0] = acc.astype(o_ref.dtype)


def _matmul(x, w, *, gain=None, epi=None, epi_tiles=None, extras=(), res=None, out_dtype, tm, tn, name):
    B, S, K = x.shape
    N = w.shape[1]
    norm = gain is not None
    grid = (B, S // tm, N // tn)
    in_specs = [pl.BlockSpec((1, tm, K), lambda b, i, j: (b, i, 0))]
    args = [x]
    if norm:
        in_specs.append(pl.BlockSpec((1, K), lambda b, i, j: (0, 0)))
        args.append(gain)
    in_specs.append(pl.BlockSpec((K, tn), lambda b, i, j: (0, j)))
    args.append(w)
    for e in extras:
        in_specs.append(pl.BlockSpec((1, tn), lambda b, i, j: (0, j)))
        args.append(e)
    if res is not None:
        in_specs.append(pl.BlockSpec((1, tm, tn), lambda b, i, j: (b, i, j)))
        args.append(res)
    body = functools.partial(_mm_kernel, norm=norm, epi=epi, epi_tiles=epi_tiles,
                             n_extra=len(extras), has_res=res is not None)
    return pl.pallas_call(
        body, grid=grid, in_specs=in_specs,
        out_specs=pl.BlockSpec((1, tm, tn), lambda b, i, j: (b, i, j)),
        out_shape=jax.ShapeDtypeStruct((B, S, N), out_dtype),
        scratch_shapes=[pltpu.VMEM((tm, K), BF16)] if norm else [],
        compiler_params=_params(3), name=name)(*args)


def _relu2_epi(acc):
    a = jnp.maximum(acc, 0.0)
    return a * a


def _head_norm_epi(acc, g_ref):
    g = g_ref[...]
    outs = []
    for c in range(acc.shape[1] // HEAD_DIM):
        sl = slice(c * HEAD_DIM, (c + 1) * HEAD_DIM)
        outs.append(_rms(acc[:, sl]) * g[:, sl])
    return jnp.concatenate(outs, axis=1)


def _causal_tiles(S):
    return [(qi, list(range(qi + 1))) for qi in range(S // ATTN_TQ)]


def _tile_masks():
    key = lax.broadcasted_iota(jnp.int32, (ATTN_TK, ATTN_TQ), 0)
    qry = lax.broadcasted_iota(jnp.int32, (ATTN_TK, ATTN_TQ), 1)
    return key, qry


def _raw_scores(k_ref, q_ref, qi, ki):
    return lax.dot_general(k_ref[ki * ATTN_TK:(ki + 1) * ATTN_TK, :],
                           q_ref[qi * ATTN_TQ:(qi + 1) * ATTN_TQ, :], _NT,
                           preferred_element_type=F32)


def _transpose_v_tile(vT_ref, v_tile, ki):
    vT_ref[:, ki * ATTN_TK:(ki + 1) * ATTN_TK] = v_tile(ki).T.astype(BF16)


def _softmax_sweep(q_ref, k_ref, v_tile, vT_ref, o_ref, *, scale, bias, S):
    tq, tk = ATTN_TQ, ATTN_TK
    c = scale * LOG2E
    key, qry = _tile_masks()
    causal = key <= qry
    blocks = _causal_tiles(S)
    order = [(qi, ki) for qi, kis in blocks for ki in kis]
    pending = [_raw_scores(k_ref, q_ref, *t) for t in order[:SCORE_LOOKAHEAD]]
    issued = SCORE_LOOKAHEAD
    _transpose_v_tile(vT_ref, v_tile, 0)
    for qi, kis in blocks:
        if qi + 1 < len(blocks):
            _transpose_v_tile(vT_ref, v_tile, qi + 1)
        m = l = acc = None
        for ki in kis:
            s_raw = pending.pop(0)
            if issued < len(order):
                pending.append(_raw_scores(k_ref, q_ref, *order[issued]))
                issued += 1
            s = s_raw * c
            b = bias(ki, qi)
            if b is not None:
                s = s + b
            if ki == qi:
                s = jnp.where(causal, s, -jnp.inf)
            s_max = jnp.max(s, axis=0, keepdims=True)
            m_new = s_max if m is None else jnp.maximum(m, s_max)
            p = jnp.exp2(s - m_new)
            p_sum = jnp.sum(p, axis=0, keepdims=True)
            pv = jnp.dot(vT_ref[:, ki * tk:(ki + 1) * tk], p.astype(BF16),
                         preferred_element_type=F32)
            if m is None:
                l, acc = p_sum, pv
            else:
                alpha = jnp.exp2(m - m_new)
                l = alpha * l + p_sum
                acc = alpha * acc + pv
            m = m_new
        o = acc / l
        o_ref[0, qi * tq:(qi + 1) * tq, :] = o.T.astype(o_ref.dtype)


def _sb_attn_kernel(q_ref, k_ref, v_ref, u_ref, o_ref, vT_ref, *, scale, S):
    tq, tk = ATTN_TQ, ATTN_TK
    c = scale * LOG2E
    key, qry = _tile_masks()
    strict = key < qry
    blocks = _causal_tiles(S)
    order = [(qi, ki) for qi, kis in blocks for ki in reversed(kis)]
    q2, k2 = q_ref.at[0], k_ref.at[0]

    def v_tile(ki):
        return v_ref[0, ki * tk:(ki + 1) * tk, :].astype(F32)

    def gate(t, z_raw):
        qi, ki = t
        z = z_raw * c
        soft = jnp.log2(1.0 + jnp.exp2(-jnp.abs(z)))
        nlk = jnp.maximum(z, 0.0) + soft
        log_beta = z - nlk
        if ki == qi:
            nlk = jnp.where(strict, nlk, 0.0)
        hi = nlk.astype(BF16)
        lo = (nlk - hi.astype(F32)).astype(BF16)
        later = jnp.dot(u_ref[...], jnp.concatenate([hi, lo], axis=0),
                        preferred_element_type=F32)
        return log_beta, later, jnp.sum(nlk, axis=0, keepdims=True)

    scored = [_raw_scores(k2, q2, *t) for t in order[:SCORE_LOOKAHEAD]]
    gated = []
    n_scored = [SCORE_LOOKAHEAD]
    n_gated = [0]

    def advance():
        if n_scored[0] < len(order):
            scored.append(_raw_scores(k2, q2, *order[n_scored[0]]))
            n_scored[0] += 1
        if n_gated[0] < len(order):
            gated.append(gate(order[n_gated[0]], scored.pop(0)))
            n_gated[0] += 1

    for _ in range(SCAN_LOOKAHEAD):
        advance()
    _transpose_v_tile(vT_ref, v_tile, 0)
    for qi, kis in blocks:
        if qi + 1 < len(blocks):
            _transpose_v_tile(vT_ref, v_tile, qi + 1)
        suf = acc = None
        for ki in reversed(kis):
            log_beta, later, col = gated.pop(0)
            advance()
            x = log_beta + later
            if suf is not None:
                x = x + suf
            a = jnp.exp2(x)
            if ki == qi:
                a = jnp.where(strict, a, 0.0)
            pv = jnp.dot(vT_ref[:, ki * tk:(ki + 1) * tk], a.astype(BF16),
                         preferred_element_type=F32)
            acc = pv if acc is None else acc + pv
            suf = -col if suf is None else suf - col
        o_ref[0, qi * tq:(qi + 1) * tq, :] = acc.T.astype(o_ref.dtype)


def _sb_attention(qkv, B, S):
    tk = ATTN_TK
    upper = (jnp.arange(tk)[None, :] > jnp.arange(tk)[:, None])
    neg_u2 = jnp.tile(jnp.where(upper, -1.0, 0.0), (1, 2)).astype(BF16)
    H = N_HEADS
    return pl.pallas_call(
        functools.partial(_sb_attn_kernel, scale=1.0 / math.sqrt(HEAD_DIM), S=S),
        grid=(B, H),
        in_specs=[pl.BlockSpec((1, S, HEAD_DIM), lambda b, h: (b, 0, h)),
                  pl.BlockSpec((1, S, HEAD_DIM), lambda b, h: (b, 0, H + h)),
                  pl.BlockSpec((1, S, HEAD_DIM), lambda b, h: (b, 0, 2 * H + h)),
                  pl.BlockSpec((tk, 2 * tk), lambda b, h: (0, 0))],
        out_specs=pl.BlockSpec((1, S, HEAD_DIM), lambda b, h: (b, 0, h)),
        out_shape=jax.ShapeDtypeStruct((B, S, ATTN_WIDTH), BF16),
        scratch_shapes=[pltpu.VMEM((HEAD_DIM, S), BF16)],
        compiler_params=_params(2), name="sb_attention")(qkv, qkv, qkv, neg_u2)


def _fox_attn_kernel(q_ref, k_ref, v_ref, cfc_ref, cfr_ref, o_ref, vT_ref, slab_ref, *, scale, S):
    h = pl.program_id(1)

    def prep(c, carry):
        r0 = pl.multiple_of(c * PREP_ROWS, PREP_ROWS)
        blk = cfc_ref[0, pl.ds(r0, PREP_ROWS), :]
        lane = lax.broadcasted_iota(jnp.int32, blk.shape, 1)
        col = jnp.sum(jnp.where(lane == h, blk, 0.0), axis=1, keepdims=True) * LOG2E
        slab_ref[pl.ds(r0, PREP_ROWS), :] = jnp.broadcast_to(col, blk.shape)
        return carry

    lax.fori_loop(0, S // PREP_ROWS, prep, 0)

    def bias(ki, qi):
        cf_q = cfr_ref[0, 0, :, qi * ATTN_TQ:(qi + 1) * ATTN_TQ] * LOG2E
        cf_k = slab_ref[ki * ATTN_TK:(ki + 1) * ATTN_TK, :]
        return cf_q - jnp.concatenate([cf_k] * (ATTN_TQ // LANES), axis=1)

    def v_tile(ki):
        return v_ref[0, ki * ATTN_TK:(ki + 1) * ATTN_TK, :].astype(F32)

    _softmax_sweep(q_ref.at[0], k_ref.at[0], v_tile, vT_ref, o_ref, scale=scale, bias=bias, S=S)


def _fox_attention(qkv, cf_col, cf_row, B, S):
    H = N_HEADS
    return pl.pallas_call(
        functools.partial(_fox_attn_kernel, scale=1.0 / math.sqrt(HEAD_DIM), S=S),
        grid=(B, H),
        in_specs=[pl.BlockSpec((1, S, HEAD_DIM), lambda b, h: (b, 0, h)),
                  pl.BlockSpec((1, S, HEAD_DIM), lambda b, h: (b, 0, H + h)),
                  pl.BlockSpec((1, S, HEAD_DIM), lambda b, h: (b, 0, 2 * H + h)),
                  pl.BlockSpec((1, S, LANES), lambda b, h: (b, 0, 0)),
                  pl.BlockSpec((1, 1, 1, S), lambda b, h: (b, h, 0, 0))],
        out_specs=pl.BlockSpec((1, S, HEAD_DIM), lambda b, h: (b, 0, h)),
        out_shape=jax.ShapeDtypeStruct((B, S, ATTN_WIDTH), BF16),
        scratch_shapes=[pltpu.VMEM((HEAD_DIM, S), BF16), pltpu.VMEM((S, LANES), F32)],
        compiler_params=_params(2), name="fox_attention")(qkv, qkv, qkv, cf_col, cf_row)


def _mla_attn_kernel(qf_ref, kn_ref, v_ref, kr_ref, cos_ref, sin_ref, qg_ref, kg_ref, o_ref,
                     q_scr, k_scr, vT_scr, *, scale, S):
    qg = qg_ref[...]
    kg = kg_ref[...]
    pad = jnp.zeros((PREP_ROWS, MLA_QK_PAD - MLA_QK), F32)

    def normed(nope, roped, g):
        ssq = (jnp.sum(nope * nope, axis=-1, keepdims=True)
               + jnp.sum(roped * roped, axis=-1, keepdims=True))
        rs = lax.rsqrt(ssq * (1.0 / MLA_QK) + EPS)
        full = jnp.concatenate([nope * rs * g[:, :MLA_NOPE], roped * rs * g[:, MLA_NOPE:MLA_QK], pad],
                               axis=1)
        return full.astype(BF16)

    def prep(c, carry):
        r0 = pl.multiple_of(c * PREP_ROWS, PREP_ROWS)
        rows = pl.ds(r0, PREP_ROWS)
        qf = qf_ref[0, rows, :]
        q_rope = _rope(qf[:, MLA_NOPE:MLA_QK], cos_ref[0, rows, :], sin_ref[0, rows, :])
        q_scr[rows, :] = normed(qf[:, :MLA_NOPE], q_rope, qg)
        k_scr[rows, :] = normed(kn_ref[0, rows, :], kr_ref[0, rows, :], kg)
        return carry

    lax.fori_loop(0, S // PREP_ROWS, prep, 0)

    def v_tile(ki):
        return v_ref[0, ki * ATTN_TK:(ki + 1) * ATTN_TK, :]

    _softmax_sweep(q_scr, k_scr, v_tile, vT_scr, o_ref, scale=scale, bias=lambda ki, qi: None, S=S)


def _mla_attention(q_full, kv, k_rope, cos2, sin2, q_gain, k_gain, B, S):
    H = N_HEADS
    return pl.pallas_call(
        functools.partial(_mla_attn_kernel, scale=1.0 / math.sqrt(MLA_QK), S=S),
        grid=(B, H),
        in_specs=[pl.BlockSpec((1, S, MLA_QK_PAD), lambda b, h: (b, 0, h)),
                  pl.BlockSpec((1, S, MLA_NOPE), lambda b, h: (b, 0, h)),
                  pl.BlockSpec((1, S, MLA_V), lambda b, h: (b, 0, H + h)),
                  pl.BlockSpec((1, S, MLA_ROPE), lambda b, h: (b, 0, 0)),
                  pl.BlockSpec((1, S, MLA_ROPE), lambda b, h: (b, 0, 0)),
                  pl.BlockSpec((1, S, MLA_ROPE), lambda b, h: (b, 0, 0)),
                  pl.BlockSpec((1, MLA_QK_PAD), lambda b, h: (0, 0)),
                  pl.BlockSpec((1, MLA_QK_PAD), lambda b, h: (0, 0))],
        out_specs=pl.BlockSpec((1, S, MLA_V), lambda b, h: (b, 0, h)),
        out_shape=jax.ShapeDtypeStruct((B, S, H * MLA_V), BF16),
        scratch_shapes=[pltpu.VMEM((S, MLA_QK_PAD), BF16), pltpu.VMEM((S, MLA_QK_PAD), BF16),
                        pltpu.VMEM((MLA_V, S), BF16)],
        compiler_params=_params(2), name="mla_attention")(
            q_full, kv, kv, k_rope, cos2, sin2, q_gain, k_gain)


def _fox_gate_kernel(x_ref, g_ref, w_ref, b_ref, tri_ref, col_ref, row_ref, carry_ref):
    @pl.when(pl.program_id(1) == 0)
    def _():
        carry_ref[...] = jnp.zeros(carry_ref.shape, F32)

    h = (_rms(x_ref[0]) * g_ref[...]).astype(BF16)
    f = jnp.dot(h, w_ref[...], preferred_element_type=F32) + b_ref[...]
    log_f = jnp.minimum(f, 0.0) - jnp.log1p(jnp.exp(-jnp.abs(f)))
    p0 = log_f.astype(BF16)
    r1 = log_f - p0.astype(F32)
    p1 = r1.astype(BF16)
    p2 = (r1 - p1.astype(F32)).astype(BF16)
    tri = tri_ref[...]
    cs = (jnp.dot(tri, p0, preferred_element_type=F32) + jnp.dot(tri, p1, preferred_element_type=F32)
          + jnp.dot(tri, p2, preferred_element_type=F32)) + carry_ref[...]
    col_ref[0] = cs
    row_ref[0] = cs.T
    carry_ref[...] = cs[cs.shape[0] - 1:, :]


def _fox_gate(x, gain, w_f, b_f, ts=512):
    B, S, K = x.shape
    tri = (jnp.arange(ts)[None, :] <= jnp.arange(ts)[:, None]).astype(BF16)
    return pl.pallas_call(
        _fox_gate_kernel, grid=(B, S // ts),
        in_specs=[pl.BlockSpec((1, ts, K), lambda b, i: (b, i, 0)),
                  pl.BlockSpec((1, K), lambda b, i: (0, 0)),
                  pl.BlockSpec((K, LANES), lambda b, i: (0, 0)),
                  pl.BlockSpec((1, LANES), lambda b, i: (0, 0)),
                  pl.BlockSpec((ts, ts), lambda b, i: (0, 0))],
        out_specs=[pl.BlockSpec((1, ts, LANES), lambda b, i: (b, i, 0)),
                   pl.BlockSpec((1, LANES, ts), lambda b, i: (b, 0, i))],
        out_shape=[jax.ShapeDtypeStruct((B, S, LANES), F32), jax.ShapeDtypeStruct((B, LANES, S), F32)],
        scratch_shapes=[pltpu.VMEM((1, LANES), F32)],
        compiler_params=_params(2), name="fox_gate")(x, gain, w_f, b_f, tri)


def _rope_table_kernel(pos_ref, inv_ref, cos_ref, sin_ref):
    ang = pos_ref[0].astype(F32) * inv_ref[...]
    cos_ref[0] = jnp.cos(ang)
    sin_ref[0] = jnp.sin(ang)


def _rope_tables(positions):
    B, S = positions.shape
    half = MLA_ROPE // 2
    inv_freq = ROPE_THETA ** (-jnp.arange(0, half, dtype=F32) * 2.0 / MLA_ROPE)
    inv2 = jnp.concatenate([inv_freq, inv_freq])[None, :]
    out = jax.ShapeDtypeStruct((B, S, MLA_ROPE), F32)
    return pl.pallas_call(
        _rope_table_kernel, grid=(B,),
        in_specs=[pl.BlockSpec((1, S, 1), lambda b: (b, 0, 0)),
                  pl.BlockSpec((1, MLA_ROPE), lambda b: (0, 0))],
        out_specs=[pl.BlockSpec((1, S, MLA_ROPE), lambda b: (b, 0, 0))] * 2,
        out_shape=[out, out], compiler_params=_params(1), name="rope_tables")(
            positions.reshape(B, S, 1), inv2)


def _mla_down_kernel(x_ref, g_ref, w_ref, qn_ref, kvn_ref, cos_ref, sin_ref, cq_ref, ckv_ref, kr_ref):
    h = (_rms(x_ref[0]) * g_ref[...]).astype(BF16)
    down = jnp.dot(h, w_ref[...], preferred_element_type=F32)
    kv0 = MLA_Q_RANK
    r0 = MLA_Q_RANK + MLA_KV_RANK
    cq_ref[0] = (_rms(down[:, :kv0]) * qn_ref[...]).astype(BF16)
    ckv_ref[0] = (_rms(down[:, kv0:r0]) * kvn_ref[...]).astype(BF16)
    kr_ref[0] = _rope(down[:, r0:r0 + MLA_ROPE], cos_ref[0], sin_ref[0])


def _mla_down(x, gain, w_pad, q_norm, kv_norm, cos2, sin2, tm=512):
    B, S, K = x.shape
    n_pad = w_pad.shape[1]
    row = lambda b, i: (b, i, 0)
    fixed = lambda b, i: (0, 0)
    return pl.pallas_call(
        _mla_down_kernel, grid=(B, S // tm),
        in_specs=[pl.BlockSpec((1, tm, K), row), pl.BlockSpec((1, K), fixed),
                  pl.BlockSpec((K, n_pad), fixed),
                  pl.BlockSpec((1, MLA_Q_RANK), fixed), pl.BlockSpec((1, MLA_KV_RANK), fixed),
                  pl.BlockSpec((1, tm, MLA_ROPE), row), pl.BlockSpec((1, tm, MLA_ROPE), row)],
        out_specs=[pl.BlockSpec((1, tm, MLA_Q_RANK), row), pl.BlockSpec((1, tm, MLA_KV_RANK), row),
                   pl.BlockSpec((1, tm, MLA_ROPE), row)],
        out_shape=[jax.ShapeDtypeStruct((B, S, MLA_Q_RANK), BF16),
                   jax.ShapeDtypeStruct((B, S, MLA_KV_RANK), BF16),
                   jax.ShapeDtypeStruct((B, S, MLA_ROPE), F32)],
        compiler_params=_params(2), name="mla_down")(x, gain, w_pad, q_norm, kv_norm, cos2, sin2)


def _sb_layer(x, gain, w_in, w_out):
    B, S, _ = x.shape
    qkv = _matmul(x, w_in.astype(BF16), gain=gain, out_dtype=BF16, tm=512, tn=1024, name="sb_qkv")
    o = _sb_attention(qkv, B, S)
    return _matmul(o, w_out.astype(BF16), res=x, out_dtype=F32, tm=512, tn=1024, name="sb_out")


def _fox_layer(x, gain, w_in, b_f, q_gain, k_gain, w_out):
    B, S, _ = x.shape
    tn = 1024
    w_qkv = w_in[:, :3 * ATTN_WIDTH].astype(BF16)
    w_f = jnp.pad(w_in[:, 3 * ATTN_WIDTH:], ((0, 0), (0, LANES - N_HEADS))).astype(BF16)
    b_pad = jnp.pad(b_f, (0, LANES - N_HEADS))[None, :]
    qkv_gain = jnp.concatenate([jnp.tile(q_gain, N_HEADS), jnp.tile(k_gain, N_HEADS),
                                jnp.ones((ATTN_WIDTH,), F32)])[None, :]
    qkv = _matmul(x, w_qkv, gain=gain, epi=_head_norm_epi, epi_tiles=2 * ATTN_WIDTH // tn,
                  extras=(qkv_gain,), out_dtype=BF16, tm=512, tn=tn, name="fox_qkv")
    cf_col, cf_rowT = _fox_gate(x, gain, w_f, b_pad)
    cf_row = cf_rowT[:, :N_HEADS, :].reshape(B, N_HEADS, 1, S)
    o = _fox_attention(qkv, cf_col, cf_row, B, S)
    return _matmul(o, w_out.astype(BF16), res=x, out_dtype=F32, tm=512, tn=1024, name="fox_out")


def _mla_layer(x, positions, gain, w_in, q_norm, kv_norm, w_uq, w_ukv, q_gain, k_gain, w_out):
    B, S, _ = x.shape
    n_down = w_in.shape[1]
    n_pad = -(-n_down // LANES) * LANES
    w_down = jnp.pad(w_in, ((0, 0), (0, n_pad - n_down))).astype(BF16)
    w_q = jnp.pad(w_uq.reshape(MLA_Q_RANK, N_HEADS, MLA_QK),
                  ((0, 0), (0, 0), (0, MLA_QK_PAD - MLA_QK))).reshape(MLA_Q_RANK, -1).astype(BF16)
    w_kv = w_ukv.reshape(MLA_KV_RANK, N_HEADS, MLA_NOPE + MLA_V)
    w_kv = jnp.concatenate([w_kv[:, :, :MLA_NOPE].reshape(MLA_KV_RANK, -1),
                            w_kv[:, :, MLA_NOPE:].reshape(MLA_KV_RANK, -1)], axis=1).astype(BF16)
    qg = jnp.pad(q_gain, (0, MLA_QK_PAD - MLA_QK))[None, :]
    kg = jnp.pad(k_gain, (0, MLA_QK_PAD - MLA_QK))[None, :]

    cos2, sin2 = _rope_tables(positions)
    c_q, c_kv, k_rope = _mla_down(x, gain, w_down, q_norm[None, :], kv_norm[None, :], cos2, sin2)
    q_full = _matmul(c_q, w_q, out_dtype=F32, tm=512, tn=1024, name="mla_uq")
    kv = _matmul(c_kv, w_kv, out_dtype=F32, tm=512, tn=1024, name="mla_ukv")
    o = _mla_attention(q_full, kv, k_rope, cos2, sin2, qg, kg, B, S)
    return _matmul(o, w_out.astype(BF16), res=x, out_dtype=F32, tm=512, tn=1024, name="mla_out")


def _mlp(x, gain, w1, w2):
    a = _matmul(x, w1.astype(BF16), gain=gain, epi=_relu2_epi, out_dtype=BF16, tm=512, tn=1024,
                name="mlp_up")
    return _matmul(a, w2.astype(BF16), res=x, out_dtype=F32, tm=512, tn=512, name="mlp_down")


def kernel(x, positions, mix_norm, mlp_norm, sb_w_in, sb_w_out, fox_w_in, fox_b_f, fox_q_gain,
           fox_k_gain, fox_w_out, mla_w_in, mla_q_norm, mla_kv_norm, mla_w_uq, mla_w_ukv,
           mla_q_gain, mla_k_gain, mla_w_out, mlp_w1, mlp_w2):
    depth = mix_norm.shape[0]
    for i in range(depth):
        kind, j = i % N_MIXERS, i // N_MIXERS
        gain = mix_norm[i][None, :]
        if kind == 0:
            x = _sb_layer(x, gain, sb_w_in[j], sb_w_out[j])
        elif kind == 1:
            x = _fox_layer(x, gain, fox_w_in[j], fox_b_f[j], fox_q_gain[j], fox_k_gain[j], fox_w_out[j])
        else:
            x = _mla_layer(x, positions, gain, mla_w_in[j], mla_q_norm[j], mla_kv_norm[j], mla_w_uq[j],
                           mla_w_ukv[j], mla_q_gain[j], mla_k_gain[j], mla_w_out[j])
        x = _mlp(x, mlp_norm[i][None, :], mlp_w1[i], mlp_w2[i])
    return x
```

```python
import functools
import math

import jax
import jax.numpy as jnp
from jax import lax
from jax.experimental import pallas as pl
from jax.experimental.pallas import tpu as pltpu

F32 = jnp.float32
BF16 = jnp.bfloat16

D_MODEL = 2048
HEAD_DIM = 128
N_HEADS = D_MODEL // HEAD_DIM
ATTN_WIDTH = N_HEADS * HEAD_DIM
D_FF = 4 * D_MODEL
MLA_Q_RANK = 3 * D_MODEL // 8
MLA_KV_RANK = D_MODEL // 4
MLA_NOPE = 128
MLA_ROPE = 64
MLA_V = 128
MLA_QK = MLA_NOPE + MLA_ROPE
MLA_QK_PAD = 256
ROPE_THETA = 10000.0
EPS = 1e-6
N_MIXERS = 3
LOG2E = math.log2(math.e)

LANES = 128
ATTN_TQ = 256
ATTN_TK = 256
PREP_ROWS = 256
SCORE_LOOKAHEAD = 5
SCAN_LOOKAHEAD = 2
MM_TM = 1024
MM_TN = 1024
MLP_TK = 2048
OUT_TM = 512
VMEM_LIMIT = 48 * 1024 * 1024

_NT = (((1,), (1,)), ((), ()))


def _params(n_grid):
    return pltpu.CompilerParams(dimension_semantics=("arbitrary",) * n_grid,
                                vmem_limit_bytes=VMEM_LIMIT)


def _rms(x):
    ms = jnp.mean(x * x, axis=-1, keepdims=True)
    return x * lax.rsqrt(ms + EPS)


def _rope(x, cos2, sin2):
    half = x.shape[-1] // 2
    rot = jnp.concatenate([-x[:, half:], x[:, :half]], axis=1)
    return x * cos2 + rot * sin2


def _mm_kernel(*refs, norm, epi, epi_tiles, n_extra, has_res):
    it = iter(refs)
    x_ref = next(it)
    g_ref = next(it) if norm else None
    w_ref = next(it)
    extra = [next(it) for _ in range(n_extra)]
    res_ref = next(it) if has_res else None
    o_ref = next(it)
    h_ref = next(it) if norm else None

    if norm:
        @pl.when(pl.program_id(2) == 0)
        def _():
            h_ref[...] = (_rms(x_ref[0]) * g_ref[...]).astype(BF16)
        h = h_ref[...]
    else:
        h = x_ref[0]
    acc = jnp.dot(h, w_ref[0].astype(BF16), preferred_element_type=F32)
    if has_res:
        acc = res_ref[0] + acc
    if epi is not None:
        active = None if epi_tiles is None else pl.program_id(2) < epi_tiles
        acc = epi(acc, *extra, active=active)
    o_ref[0] = acc.astype(o_ref.dtype)


def _matmul(x, w, layer=0, *, n_cols=None, gain=None, epi=None, epi_tiles=None, extras=(), res=None,
            out_dtype, tm=MM_TM, tn=MM_TN, name):
    B, S, K = x.shape
    N = w.shape[2] if n_cols is None else n_cols
    norm = gain is not None
    grid = (B, S // tm, N // tn)
    in_specs = [pl.BlockSpec((1, tm, K), lambda b, i, j: (b, i, 0))]
    args = [x]
    if norm:
        in_specs.append(pl.BlockSpec((1, K), lambda b, i, j: (0, 0)))
        args.append(gain)
    in_specs.append(pl.BlockSpec((1, K, tn), lambda b, i, j: (layer, 0, j)))
    args.append(w)
    for e in extras:
        in_specs.append(pl.BlockSpec((1, tn), lambda b, i, j: (0, j)))
        args.append(e)
    if res is not None:
        in_specs.append(pl.BlockSpec((1, tm, tn), lambda b, i, j: (b, i, j)))
        args.append(res)
    body = functools.partial(_mm_kernel, norm=norm, epi=epi, epi_tiles=epi_tiles,
                             n_extra=len(extras), has_res=res is not None)
    return pl.pallas_call(
        body, grid=grid, in_specs=in_specs,
        out_specs=pl.BlockSpec((1, tm, tn), lambda b, i, j: (b, i, j)),
        out_shape=jax.ShapeDtypeStruct((B, S, N), out_dtype),
        scratch_shapes=[pltpu.VMEM((tm, K), BF16)] if norm else [],
        compiler_params=_params(3), name=name)(*args)


def _mm_ksplit_kernel(x_ref, w_ref, res_ref, o_ref):
    @pl.when(pl.program_id(3) == 0)
    def _():
        o_ref[0] = res_ref[0]

    o_ref[0] += jnp.dot(x_ref[0], w_ref[0].astype(BF16), preferred_element_type=F32)


def _matmul_ksplit(x, w, layer, res, *, tm=MM_TM, tn=MM_TN, tk=MLP_TK, name):
    B, S, K = x.shape
    N = w.shape[2]
    return pl.pallas_call(
        _mm_ksplit_kernel, grid=(B, S // tm, N // tn, K // tk),
        in_specs=[pl.BlockSpec((1, tm, tk), lambda b, i, j, k: (b, i, k)),
                  pl.BlockSpec((1, tk, tn), lambda b, i, j, k: (layer, k, j)),
                  pl.BlockSpec((1, tm, tn), lambda b, i, j, k: (b, i, j))],
        out_specs=pl.BlockSpec((1, tm, tn), lambda b, i, j, k: (b, i, j)),
        out_shape=jax.ShapeDtypeStruct((B, S, N), F32),
        compiler_params=_params(4), name=name)(x, w, res)


def _relu2_epi(acc, active=None):
    a = jnp.maximum(acc, 0.0)
    return a * a


def _head_norm_epi(acc, g_ref, active=None):
    g = g_ref[...]
    outs = []
    for c in range(acc.shape[1] // HEAD_DIM):
        sl = slice(c * HEAD_DIM, (c + 1) * HEAD_DIM)
        blk = acc[:, sl]
        inv = lax.rsqrt(jnp.mean(blk * blk, axis=-1, keepdims=True) + EPS)
        if active is not None:
            inv = jnp.where(active, inv, 1.0)
        outs.append(blk * inv * g[:, sl])
    return jnp.concatenate(outs, axis=1)


def _causal_tiles(S):
    return [(qi, list(range(qi + 1))) for qi in range(S // ATTN_TQ)]


def _tile_masks():
    key = lax.broadcasted_iota(jnp.int32, (ATTN_TK, ATTN_TQ), 0)
    qry = lax.broadcasted_iota(jnp.int32, (ATTN_TK, ATTN_TQ), 1)
    return key, qry


def _raw_scores(k_ref, q_ref, qi, ki):
    return lax.dot_general(k_ref[ki * ATTN_TK:(ki + 1) * ATTN_TK, :],
                           q_ref[qi * ATTN_TQ:(qi + 1) * ATTN_TQ, :], _NT,
                           preferred_element_type=F32)


def _transpose_v_tile(vT_ref, v_tile, ki):
    vT_ref[:, ki * ATTN_TK:(ki + 1) * ATTN_TK] = v_tile(ki).T.astype(BF16)


def _softmax_sweep(q_ref, k_ref, v_tile, vT_ref, o_ref, *, scale, bias, S):
    tq, tk = ATTN_TQ, ATTN_TK
    c = scale * LOG2E
    key, qry = _tile_masks()
    causal = key <= qry
    blocks = _causal_tiles(S)
    order = [(qi, ki) for qi, kis in blocks for ki in kis]
    pending = [_raw_scores(k_ref, q_ref, *t) for t in order[:SCORE_LOOKAHEAD]]
    issued = SCORE_LOOKAHEAD
    _transpose_v_tile(vT_ref, v_tile, 0)
    for qi, kis in blocks:
        if qi + 1 < len(blocks):
            _transpose_v_tile(vT_ref, v_tile, qi + 1)
        m = l = acc = None
        for ki in kis:
            s_raw = pending.pop(0)
            if issued < len(order):
                pending.append(_raw_scores(k_ref, q_ref, *order[issued]))
                issued += 1
            s = s_raw * c
            b = bias(ki, qi)
            if b is not None:
                s = s + b
            if ki == qi:
                s = jnp.where(causal, s, -jnp.inf)
            s_max = jnp.max(s, axis=0, keepdims=True)
            m_new = s_max if m is None else jnp.maximum(m, s_max)
            p = jnp.exp2(s - m_new)
            p_sum = jnp.sum(p, axis=0, keepdims=True)
            pv = jnp.dot(vT_ref[:, ki * tk:(ki + 1) * tk], p.astype(BF16),
                         preferred_element_type=F32)
            if m is None:
                l, acc = p_sum, pv
            else:
                alpha = jnp.exp2(m - m_new)
                l = alpha * l + p_sum
                acc = alpha * acc + pv
            m = m_new
        o = acc / l
        o_ref[0, qi * tq:(qi + 1) * tq, :] = o.T.astype(o_ref.dtype)


def _sb_attn_kernel(q_ref, k_ref, v_ref, u_ref, o_ref, vT_ref, *, scale, S):
    tq, tk = ATTN_TQ, ATTN_TK
    c = scale * LOG2E
    key, qry = _tile_masks()
    strict = key < qry
    blocks = _causal_tiles(S)
    order = [(qi, ki) for qi, kis in blocks for ki in reversed(kis)]
    q2, k2 = q_ref.at[0], k_ref.at[0]

    def v_tile(ki):
        return v_ref[0, ki * tk:(ki + 1) * tk, :].astype(F32)

    def gate(t, z_raw):
        qi, ki = t
        z = z_raw * c
        soft = jnp.log2(1.0 + jnp.exp2(-jnp.abs(z)))
        nlk = jnp.maximum(z, 0.0) + soft
        log_beta = z - nlk
        if ki == qi:
            nlk = jnp.where(strict, nlk, 0.0)
        hi = nlk.astype(BF16)
        lo = (nlk - hi.astype(F32)).astype(BF16)
        later = jnp.dot(u_ref[...], jnp.concatenate([hi, lo], axis=0),
                        preferred_element_type=F32)
        return log_beta, later, jnp.sum(nlk, axis=0, keepdims=True)

    scored = [_raw_scores(k2, q2, *t) for t in order[:SCORE_LOOKAHEAD]]
    gated = []
    n_scored = [SCORE_LOOKAHEAD]
    n_gated = [0]

    def advance():
        if n_scored[0] < len(order):
            scored.append(_raw_scores(k2, q2, *order[n_scored[0]]))
            n_scored[0] += 1
        if n_gated[0] < len(order):
            gated.append(gate(order[n_gated[0]], scored.pop(0)))
            n_gated[0] += 1

    for _ in range(SCAN_LOOKAHEAD):
        advance()
    _transpose_v_tile(vT_ref, v_tile, 0)
    for qi, kis in blocks:
        if qi + 1 < len(blocks):
            _transpose_v_tile(vT_ref, v_tile, qi + 1)
        suf = acc = None
        for ki in reversed(kis):
            log_beta, later, col = gated.pop(0)
            advance()
            x = log_beta + later
            if suf is not None:
                x = x + suf
            a = jnp.exp2(x)
            if ki == qi:
                a = jnp.where(strict, a, 0.0)
            pv = jnp.dot(vT_ref[:, ki * tk:(ki + 1) * tk], a.astype(BF16),
                         preferred_element_type=F32)
            acc = pv if acc is None else acc + pv
            suf = -col if suf is None else suf - col
        o_ref[0, qi * tq:(qi + 1) * tq, :] = acc.T.astype(o_ref.dtype)


def _sb_attention(qkv, B, S):
    tk = ATTN_TK
    upper = (jnp.arange(tk)[None, :] > jnp.arange(tk)[:, None])
    neg_u2 = jnp.tile(jnp.where(upper, -1.0, 0.0), (1, 2)).astype(BF16)
    H = N_HEADS
    return pl.pallas_call(
        functools.partial(_sb_attn_kernel, scale=1.0 / math.sqrt(HEAD_DIM), S=S),
        grid=(B, H),
        in_specs=[pl.BlockSpec((1, S, HEAD_DIM), lambda b, h: (b, 0, h)),
                  pl.BlockSpec((1, S, HEAD_DIM), lambda b, h: (b, 0, H + h)),
                  pl.BlockSpec((1, S, HEAD_DIM), lambda b, h: (b, 0, 2 * H + h)),
                  pl.BlockSpec((tk, 2 * tk), lambda b, h: (0, 0))],
        out_specs=pl.BlockSpec((1, S, HEAD_DIM), lambda b, h: (b, 0, h)),
        out_shape=jax.ShapeDtypeStruct((B, S, ATTN_WIDTH), BF16),
        scratch_shapes=[pltpu.VMEM((HEAD_DIM, S), BF16)],
        compiler_params=_params(2), name="sb_attention")(qkv, qkv, qkv, neg_u2)


def _fox_attn_kernel(q_ref, k_ref, v_ref, cfc_ref, cfr_ref, o_ref, vT_ref, slab_ref, *, scale, S):
    h = pl.program_id(1)

    def prep(c, carry):
        r0 = pl.multiple_of(c * PREP_ROWS, PREP_ROWS)
        blk = cfc_ref[0, pl.ds(r0, PREP_ROWS), :]
        lane = lax.broadcasted_iota(jnp.int32, blk.shape, 1)
        col = jnp.sum(jnp.where(lane == h, blk, 0.0), axis=1, keepdims=True) * LOG2E
        slab_ref[pl.ds(r0, PREP_ROWS), :] = jnp.broadcast_to(col, blk.shape)
        return carry

    lax.fori_loop(0, S // PREP_ROWS, prep, 0)

    def bias(ki, qi):
        cf_q = cfr_ref[0, 0, :, qi * ATTN_TQ:(qi + 1) * ATTN_TQ] * LOG2E
        cf_k = slab_ref[ki * ATTN_TK:(ki + 1) * ATTN_TK, :]
        return cf_q - jnp.concatenate([cf_k] * (ATTN_TQ // LANES), axis=1)

    def v_tile(ki):
        return v_ref[0, ki * ATTN_TK:(ki + 1) * ATTN_TK, :].astype(F32)

    _softmax_sweep(q_ref.at[0], k_ref.at[0], v_tile, vT_ref, o_ref, scale=scale, bias=bias, S=S)


def _fox_attention(qkv, cf_col, cf_row, B, S):
    H = N_HEADS
    return pl.pallas_call(
        functools.partial(_fox_attn_kernel, scale=1.0 / math.sqrt(HEAD_DIM), S=S),
        grid=(B, H),
        in_specs=[pl.BlockSpec((1, S, HEAD_DIM), lambda b, h: (b, 0, h)),
                  pl.BlockSpec((1, S, HEAD_DIM), lambda b, h: (b, 0, H + h)),
                  pl.BlockSpec((1, S, HEAD_DIM), lambda b, h: (b, 0, 2 * H + h)),
                  pl.BlockSpec((1, S, LANES), lambda b, h: (b, 0, 0)),
                  pl.BlockSpec((1, 1, 1, S), lambda b, h: (b, h, 0, 0))],
        out_specs=pl.BlockSpec((1, S, HEAD_DIM), lambda b, h: (b, 0, h)),
        out_shape=jax.ShapeDtypeStruct((B, S, ATTN_WIDTH), BF16),
        scratch_shapes=[pltpu.VMEM((HEAD_DIM, S), BF16), pltpu.VMEM((S, LANES), F32)],
        compiler_params=_params(2), name="fox_attention")(qkv, qkv, qkv, cf_col, cf_row)


def _mla_attn_kernel(qf_ref, kn_ref, v_ref, kr_ref, cos_ref, sin_ref, qg_ref, kg_ref, o_ref,
                     q_scr, k_scr, vT_scr, *, scale, S):
    qg = qg_ref[...]
    kg = kg_ref[...]
    pad = jnp.zeros((PREP_ROWS, MLA_QK_PAD - MLA_QK), F32)

    def normed(nope, roped, g):
        full = jnp.concatenate([nope, roped, pad], axis=1)
        ms = jnp.sum(full * full, axis=-1, keepdims=True) * (1.0 / MLA_QK)
        return (full * lax.rsqrt(ms + EPS) * g).astype(BF16)

    def prep(c, carry):
        r0 = pl.multiple_of(c * PREP_ROWS, PREP_ROWS)
        rows = pl.ds(r0, PREP_ROWS)
        qf = qf_ref[0, rows, :]
        q_rope = _rope(qf[:, MLA_NOPE:MLA_QK], cos_ref[0, rows, :], sin_ref[0, rows, :])
        q_scr[rows, :] = normed(qf[:, :MLA_NOPE], q_rope, qg)
        k_scr[rows, :] = normed(kn_ref[0, rows, :], kr_ref[0, rows, :], kg)
        return carry

    lax.fori_loop(0, S // PREP_ROWS, prep, 0)

    def v_tile(ki):
        return v_ref[0, ki * ATTN_TK:(ki + 1) * ATTN_TK, :]

    _softmax_sweep(q_scr, k_scr, v_tile, vT_scr, o_ref, scale=scale, bias=lambda ki, qi: None, S=S)


def _mla_attention(q_full, kv, k_rope, cos2, sin2, q_gain, k_gain, B, S):
    H = N_HEADS
    return pl.pallas_call(
        functools.partial(_mla_attn_kernel, scale=1.0 / math.sqrt(MLA_QK), S=S),
        grid=(B, H),
        in_specs=[pl.BlockSpec((1, S, MLA_QK_PAD), lambda b, h: (b, 0, h)),
                  pl.BlockSpec((1, S, MLA_NOPE), lambda b, h: (b, 0, h)),
                  pl.BlockSpec((1, S, MLA_V), lambda b, h: (b, 0, H + h)),
                  pl.BlockSpec((1, S, MLA_ROPE), lambda b, h: (b, 0, 0)),
                  pl.BlockSpec((1, S, MLA_ROPE), lambda b, h: (b, 0, 0)),
                  pl.BlockSpec((1, S, MLA_ROPE), lambda b, h: (b, 0, 0)),
                  pl.BlockSpec((1, MLA_QK_PAD), lambda b, h: (0, 0)),
                  pl.BlockSpec((1, MLA_QK_PAD), lambda b, h: (0, 0))],
        out_specs=pl.BlockSpec((1, S, MLA_V), lambda b, h: (b, 0, h)),
        out_shape=jax.ShapeDtypeStruct((B, S, H * MLA_V), BF16),
        scratch_shapes=[pltpu.VMEM((S, MLA_QK_PAD), BF16), pltpu.VMEM((S, MLA_QK_PAD), BF16),
                        pltpu.VMEM((MLA_V, S), BF16)],
        compiler_params=_params(2), name="mla_attention")(
            q_full, kv, kv, k_rope, cos2, sin2, q_gain, k_gain)


def _fox_gate_kernel(x_ref, g_ref, w_ref, b_ref, tri_ref, col_ref, row_ref, carry_ref):
    @pl.when(pl.program_id(1) == 0)
    def _():
        carry_ref[...] = jnp.zeros(carry_ref.shape, F32)

    h = (_rms(x_ref[0]) * g_ref[...]).astype(BF16)
    f = jnp.dot(h, w_ref[...], preferred_element_type=F32) + b_ref[...]
    log_f = jnp.minimum(f, 0.0) - jnp.log1p(jnp.exp(-jnp.abs(f)))
    p0 = log_f.astype(BF16)
    r1 = log_f - p0.astype(F32)
    p1 = r1.astype(BF16)
    p2 = (r1 - p1.astype(F32)).astype(BF16)
    tri = tri_ref[...]
    cs = (jnp.dot(tri, p0, preferred_element_type=F32) + jnp.dot(tri, p1, preferred_element_type=F32)
          + jnp.dot(tri, p2, preferred_element_type=F32)) + carry_ref[...]
    col_ref[0] = cs
    row_ref[0] = cs.T
    carry_ref[...] = cs[cs.shape[0] - 1:, :]


def _fox_gate(x, gain, w_f, b_f, ts=512):
    B, S, K = x.shape
    tri = (jnp.arange(ts)[None, :] <= jnp.arange(ts)[:, None]).astype(BF16)
    return pl.pallas_call(
        _fox_gate_kernel, grid=(B, S // ts),
        in_specs=[pl.BlockSpec((1, ts, K), lambda b, i: (b, i, 0)),
                  pl.BlockSpec((1, K), lambda b, i: (0, 0)),
                  pl.BlockSpec((K, LANES), lambda b, i: (0, 0)),
                  pl.BlockSpec((1, LANES), lambda b, i: (0, 0)),
                  pl.BlockSpec((ts, ts), lambda b, i: (0, 0))],
        out_specs=[pl.BlockSpec((1, ts, LANES), lambda b, i: (b, i, 0)),
                   pl.BlockSpec((1, LANES, ts), lambda b, i: (b, 0, i))],
        out_shape=[jax.ShapeDtypeStruct((B, S, LANES), F32), jax.ShapeDtypeStruct((B, LANES, S), F32)],
        scratch_shapes=[pltpu.VMEM((1, LANES), F32)],
        compiler_params=_params(2), name="fox_gate")(x, gain, w_f, b_f, tri)


def _rope_table_kernel(pos_ref, inv_ref, cos_ref, sin_ref):
    ang = pos_ref[0].astype(F32) * inv_ref[...]
    cos_ref[0] = jnp.cos(ang)
    sin_ref[0] = jnp.sin(ang)


def _rope_tables(positions):
    B, S = positions.shape
    half = MLA_ROPE // 2
    inv_freq = ROPE_THETA ** (-jnp.arange(0, half, dtype=F32) * 2.0 / MLA_ROPE)
    inv2 = jnp.concatenate([inv_freq, inv_freq])[None, :]
    out = jax.ShapeDtypeStruct((B, S, MLA_ROPE), F32)
    return pl.pallas_call(
        _rope_table_kernel, grid=(B,),
        in_specs=[pl.BlockSpec((1, S, 1), lambda b: (b, 0, 0)),
                  pl.BlockSpec((1, MLA_ROPE), lambda b: (0, 0))],
        out_specs=[pl.BlockSpec((1, S, MLA_ROPE), lambda b: (b, 0, 0))] * 2,
        out_shape=[out, out], compiler_params=_params(1), name="rope_tables")(
            positions.reshape(B, S, 1), inv2)


def _mla_down_kernel(x_ref, g_ref, w_ref, qn_ref, kvn_ref, cos_ref, sin_ref, cq_ref, ckv_ref, kr_ref):
    h = (_rms(x_ref[0]) * g_ref[...]).astype(BF16)
    down = jnp.dot(h, w_ref[...], preferred_element_type=F32)
    kv0 = MLA_Q_RANK
    r0 = MLA_Q_RANK + MLA_KV_RANK
    cq_ref[0] = (_rms(down[:, :kv0]) * qn_ref[...]).astype(BF16)
    ckv_ref[0] = (_rms(down[:, kv0:r0]) * kvn_ref[...]).astype(BF16)
    kr_ref[0] = _rope(down[:, r0:r0 + MLA_ROPE], cos_ref[0], sin_ref[0])


def _mla_down(x, gain, w_pad, q_norm, kv_norm, cos2, sin2, tm=512):
    B, S, K = x.shape
    n_pad = w_pad.shape[1]
    row = lambda b, i: (b, i, 0)
    fixed = lambda b, i: (0, 0)
    return pl.pallas_call(
        _mla_down_kernel, grid=(B, S // tm),
        in_specs=[pl.BlockSpec((1, tm, K), row), pl.BlockSpec((1, K), fixed),
                  pl.BlockSpec((K, n_pad), fixed),
                  pl.BlockSpec((1, MLA_Q_RANK), fixed), pl.BlockSpec((1, MLA_KV_RANK), fixed),
                  pl.BlockSpec((1, tm, MLA_ROPE), row), pl.BlockSpec((1, tm, MLA_ROPE), row)],
        out_specs=[pl.BlockSpec((1, tm, MLA_Q_RANK), row), pl.BlockSpec((1, tm, MLA_KV_RANK), row),
                   pl.BlockSpec((1, tm, MLA_ROPE), row)],
        out_shape=[jax.ShapeDtypeStruct((B, S, MLA_Q_RANK), BF16),
                   jax.ShapeDtypeStruct((B, S, MLA_KV_RANK), BF16),
                   jax.ShapeDtypeStruct((B, S, MLA_ROPE), F32)],
        compiler_params=_params(2), name="mla_down")(x, gain, w_pad, q_norm, kv_norm, cos2, sin2)


def _out_proj(o, w_out, x, *, name):
    return _matmul(o, w_out.astype(BF16)[None], res=x, out_dtype=F32, tm=OUT_TM, tn=w_out.shape[1],
                   name=name)


def _sb_layer(x, gain, w_in, w_out, j):
    B, S, _ = x.shape
    qkv = _matmul(x, w_in, j, gain=gain, out_dtype=BF16, name="sb_qkv")
    o = _sb_attention(qkv, B, S)
    return _out_proj(o, w_out[j], x, name="sb_out")


def _fox_layer(x, gain, w_in, b_f, q_gain, k_gain, w_out, j):
    B, S, _ = x.shape
    w_f = jnp.pad(w_in[j, :, 3 * ATTN_WIDTH:], ((0, 0), (0, LANES - N_HEADS))).astype(BF16)
    b_pad = jnp.pad(b_f, (0, LANES - N_HEADS))[None, :]
    qkv_gain = jnp.concatenate([jnp.tile(q_gain, N_HEADS), jnp.tile(k_gain, N_HEADS),
                                jnp.ones((ATTN_WIDTH,), F32)])[None, :]
    qkv = _matmul(x, w_in, j, n_cols=3 * ATTN_WIDTH, gain=gain, epi=_head_norm_epi,
                  epi_tiles=2 * ATTN_WIDTH // MM_TN, extras=(qkv_gain,), out_dtype=BF16, name="fox_qkv")
    cf_col, cf_rowT = _fox_gate(x, gain, w_f, b_pad)
    cf_row = cf_rowT[:, :N_HEADS, :].reshape(B, N_HEADS, 1, S)
    o = _fox_attention(qkv, cf_col, cf_row, B, S)
    return _out_proj(o, w_out[j], x, name="fox_out")


def _mla_layer(x, positions, gain, w_in, q_norm, kv_norm, w_uq, w_ukv, q_gain, k_gain, w_out_all, j):
    B, S, _ = x.shape
    n_down = w_in.shape[1]
    n_pad = -(-n_down // LANES) * LANES
    w_down = jnp.pad(w_in, ((0, 0), (0, n_pad - n_down))).astype(BF16)
    w_q = jnp.pad(w_uq.reshape(MLA_Q_RANK, N_HEADS, MLA_QK),
                  ((0, 0), (0, 0), (0, MLA_QK_PAD - MLA_QK))).reshape(MLA_Q_RANK, -1).astype(BF16)
    w_kv = w_ukv.reshape(MLA_KV_RANK, N_HEADS, MLA_NOPE + MLA_V)
    w_kv = jnp.concatenate([w_kv[:, :, :MLA_NOPE].reshape(MLA_KV_RANK, -1),
                            w_kv[:, :, MLA_NOPE:].reshape(MLA_KV_RANK, -1)], axis=1).astype(BF16)
    qg = jnp.pad(q_gain, (0, MLA_QK_PAD - MLA_QK))[None, :]
    kg = jnp.pad(k_gain, (0, MLA_QK_PAD - MLA_QK))[None, :]

    cos2, sin2 = _rope_tables(positions)
    c_q, c_kv, k_rope = _mla_down(x, gain, w_down, q_norm[None, :], kv_norm[None, :], cos2, sin2)
    q_full = _matmul(c_q, w_q[None], out_dtype=F32, name="mla_uq")
    kv = _matmul(c_kv, w_kv[None], out_dtype=F32, name="mla_ukv")
    o = _mla_attention(q_full, kv, k_rope, cos2, sin2, qg, kg, B, S)
    return _out_proj(o, w_out_all[j], x, name="mla_out")


def _mlp(x, gain, w1, w2, i):
    a = _matmul(x, w1, i, gain=gain, epi=_relu2_epi, out_dtype=BF16, name="mlp_up")
    return _matmul_ksplit(a, w2[i].astype(BF16)[None], 0, x, name="mlp_down")


def kernel(x, positions, mix_norm, mlp_norm, sb_w_in, sb_w_out, fox_w_in, fox_b_f, fox_q_gain,
           fox_k_gain, fox_w_out, mla_w_in, mla_q_norm, mla_kv_norm, mla_w_uq, mla_w_ukv,
           mla_q_gain, mla_k_gain, mla_w_out, mlp_w1, mlp_w2):
    depth = mix_norm.shape[0]
    for i in range(depth):
        kind, j = i % N_MIXERS, i // N_MIXERS
        gain = mix_norm[i][None, :]
        if kind == 0:
            x = _sb_layer(x, gain, sb_w_in, sb_w_out, j)
        elif kind == 1:
            x = _fox_layer(x, gain, fox_w_in, fox_b_f[j], fox_q_gain[j], fox_k_gain[j], fox_w_out, j)
        else:
            x = _mla_layer(x, positions, gain, mla_w_in[j], mla_q_norm[j], mla_kv_norm[j], mla_w_uq[j],
                           mla_w_ukv[j], mla_q_gain[j], mla_k_gain[j], mla_w_out, j)
        x = _mlp(x, mlp_norm[i][None, :], mlp_w1, mlp_w2, i)
    return x
```

```python
import functools
import math

import jax
import jax.numpy as jnp
from jax import lax
from jax.experimental import pallas as pl
from jax.experimental.pallas import tpu as pltpu

F32 = jnp.float32
BF16 = jnp.bfloat16

D_MODEL = 2048
HEAD_DIM = 128
N_HEADS = D_MODEL // HEAD_DIM
ATTN_WIDTH = N_HEADS * HEAD_DIM
D_FF = 4 * D_MODEL
MLA_Q_RANK = 3 * D_MODEL // 8
MLA_KV_RANK = D_MODEL // 4
MLA_NOPE = 128
MLA_ROPE = 64
MLA_V = 128
MLA_QK = MLA_NOPE + MLA_ROPE
MLA_QK_PAD = 256
ROPE_THETA = 10000.0
EPS = 1e-6
N_MIXERS = 3
LOG2E = math.log2(math.e)

LANES = 128
ATTN_TQ = 256
ATTN_TK = 256
SCORE_LOOKAHEAD = 5
ONES_ROWS = 16
PREP_AHEAD = 4
SCAN_LOOKAHEAD = 2
MM_TM = 1024
MM_TN = 1024
MLP_TK = 2048
OUT_TM = 512
VMEM_LIMIT = 48 * 1024 * 1024

_NT = (((1,), (1,)), ((), ()))


def _params(n_grid):
    return pltpu.CompilerParams(dimension_semantics=("arbitrary",) * n_grid,
                                vmem_limit_bytes=VMEM_LIMIT)


def _rms(x):
    ms = jnp.mean(x * x, axis=-1, keepdims=True)
    return x * lax.rsqrt(ms + EPS)


def _rope(x, cos2, sin2):
    half = x.shape[-1] // 2
    rot = jnp.concatenate([-x[:, half:], x[:, :half]], axis=1)
    return x * cos2 + rot * sin2


def _mm_kernel(*refs, norm, w_rows, epi, epi_tiles, n_extra, has_res):
    it = iter(refs)
    x_ref = next(it)
    g_ref = next(it) if norm else None
    w_ref = next(it)
    extra = [next(it) for _ in range(n_extra)]
    res_ref = next(it) if has_res else None
    o_ref = next(it)
    h_ref = next(it) if norm else None

    if norm:
        @pl.when(pl.program_id(2) == 0)
        def _():
            h_ref[...] = (_rms(x_ref[0]) * g_ref[...]).astype(BF16)
        h = h_ref[...]
    else:
        h = x_ref[0]
    w = w_ref[0].astype(BF16)
    if w_rows:
        acc = lax.dot_general(h, w, _NT, preferred_element_type=F32)
    else:
        acc = jnp.dot(h, w, preferred_element_type=F32)
    if has_res:
        acc = res_ref[0] + acc
    if epi is not None:
        active = None if epi_tiles is None else pl.program_id(2) < epi_tiles
        acc = epi(acc, *extra, active=active)
    o_ref[0] = acc.astype(o_ref.dtype)


def _matmul(x, w, layer=0, *, w_rows=False, n_cols=None, gain=None, epi=None, epi_tiles=None, extras=(),
            res=None, out_dtype, tm=MM_TM, tn=MM_TN, name):
    B, S, K = x.shape
    N = w.shape[1 if w_rows else 2] if n_cols is None else n_cols
    norm = gain is not None
    grid = (B, S // tm, N // tn)
    in_specs = [pl.BlockSpec((1, tm, K), lambda b, i, j: (b, i, 0))]
    args = [x]
    if norm:
        in_specs.append(pl.BlockSpec((1, K), lambda b, i, j: (0, 0)))
        args.append(gain)
    if w_rows:
        in_specs.append(pl.BlockSpec((1, tn, K), lambda b, i, j: (layer, j, 0)))
    else:
        in_specs.append(pl.BlockSpec((1, K, tn), lambda b, i, j: (layer, 0, j)))
    args.append(w)
    for e in extras:
        in_specs.append(pl.BlockSpec((1, tn), lambda b, i, j: (0, j)))
        args.append(e)
    if res is not None:
        in_specs.append(pl.BlockSpec((1, tm, tn), lambda b, i, j: (b, i, j)))
        args.append(res)
    body = functools.partial(_mm_kernel, norm=norm, w_rows=w_rows, epi=epi, epi_tiles=epi_tiles,
                             n_extra=len(extras), has_res=res is not None)
    return pl.pallas_call(
        body, grid=grid, in_specs=in_specs,
        out_specs=pl.BlockSpec((1, tm, tn), lambda b, i, j: (b, i, j)),
        out_shape=jax.ShapeDtypeStruct((B, S, N), out_dtype),
        scratch_shapes=[pltpu.VMEM((tm, K), BF16)] if norm else [],
        compiler_params=_params(3), name=name)(*args)


def _mm_ksplit_kernel(x_ref, w_ref, res_ref, o_ref):
    @pl.when(pl.program_id(3) == 0)
    def _():
        o_ref[0] = res_ref[0]

    o_ref[0] += jnp.dot(x_ref[0], w_ref[0].astype(BF16), preferred_element_type=F32)


def _matmul_ksplit(x, w, layer, res, *, tm=MM_TM, tn=MM_TN, tk=MLP_TK, name):
    B, S, K = x.shape
    N = w.shape[2]
    return pl.pallas_call(
        _mm_ksplit_kernel, grid=(B, S // tm, N // tn, K // tk),
        in_specs=[pl.BlockSpec((1, tm, tk), lambda b, i, j, k: (b, i, k)),
                  pl.BlockSpec((1, tk, tn), lambda b, i, j, k: (layer, k, j)),
                  pl.BlockSpec((1, tm, tn), lambda b, i, j, k: (b, i, j))],
        out_specs=pl.BlockSpec((1, tm, tn), lambda b, i, j, k: (b, i, j)),
        out_shape=jax.ShapeDtypeStruct((B, S, N), F32),
        compiler_params=_params(4), name=name)(x, w, res)


def _relu2_epi(acc, active=None):
    a = jnp.maximum(acc, 0.0)
    return a * a


def _head_norm_epi(acc, g_ref, active=None):
    g = g_ref[...]
    outs = []
    for c in range(acc.shape[1] // HEAD_DIM):
        sl = slice(c * HEAD_DIM, (c + 1) * HEAD_DIM)
        blk = acc[:, sl]
        inv = lax.rsqrt(jnp.mean(blk * blk, axis=-1, keepdims=True) + EPS)
        if active is not None:
            inv = jnp.where(active, inv, 1.0)
        outs.append(blk * inv * g[:, sl])
    return jnp.concatenate(outs, axis=1)


def _causal_tiles(S):
    return [(qi, list(range(qi + 1))) for qi in range(S // ATTN_TQ)]


def _tile_masks():
    key = lax.broadcasted_iota(jnp.int32, (ATTN_TK, ATTN_TQ), 0)
    qry = lax.broadcasted_iota(jnp.int32, (ATTN_TK, ATTN_TQ), 1)
    return key, qry


def _raw_scores(k_ref, q_ref, qi, ki):
    return lax.dot_general(k_ref[ki * ATTN_TK:(ki + 1) * ATTN_TK, :],
                           q_ref[qi * ATTN_TQ:(qi + 1) * ATTN_TQ, :], _NT,
                           preferred_element_type=F32)


def _transpose_v_tile(vT_ref, v_tile, ki):
    vt = v_tile(ki).T.astype(BF16)
    vT_ref[0:vt.shape[0], ki * ATTN_TK:(ki + 1) * ATTN_TK] = vt


class _score_issuer:
    def __init__(self, k_ref, q_ref, prep, n_blocks):
        self.k_ref, self.q_ref, self.prep, self.n_blocks = k_ref, q_ref, prep, n_blocks
        self.ready = 0
        for _ in range(min(PREP_AHEAD, n_blocks)):
            self._prepare_next()

    def _prepare_next(self):
        self.prep(self.ready)
        self.ready += 1

    def prepare_ahead(self, qi):
        if qi + PREP_AHEAD < self.n_blocks:
            self._prepare_next()

    def __call__(self, tile):
        qi, ki = tile
        assert max(qi, ki) < self.ready, "score tile issued before its rows were prepared"
        return _raw_scores(self.k_ref, self.q_ref, qi, ki)


def _softmax_sweep(q_ref, k_ref, vT_ref, o_ref, *, prep, scale, bias, S):
    tq, tk = ATTN_TQ, ATTN_TK
    c = scale * LOG2E
    key, qry = _tile_masks()
    causal = key <= qry
    blocks = _causal_tiles(S)
    order = [(qi, ki) for qi, kis in blocks for ki in kis]
    dv = vT_ref.shape[0] - ONES_ROWS
    vT_ref[dv:, :] = jnp.ones((ONES_ROWS, S), BF16)
    issue = _score_issuer(k_ref, q_ref, prep, len(blocks))
    pending = [issue(t) for t in order[:SCORE_LOOKAHEAD]]
    issued = SCORE_LOOKAHEAD
    for qi, kis in blocks:
        issue.prepare_ahead(qi)
        m = acc = None
        for ki in kis:
            s_raw = pending.pop(0)
            if issued < len(order):
                pending.append(issue(order[issued]))
                issued += 1
            s = s_raw * c
            b = bias(ki, qi)
            if b is not None:
                s = s + b
            if ki == qi:
                s = jnp.where(causal, s, -jnp.inf)
            s_max = jnp.max(s, axis=0, keepdims=True)
            m_new = s_max if m is None else jnp.maximum(m, s_max)
            p = jnp.exp2(s - m_new)
            pv = jnp.dot(vT_ref[:, ki * tk:(ki + 1) * tk], p.astype(BF16),
                         preferred_element_type=F32)
            acc = pv if m is None else jnp.exp2(m - m_new) * acc + pv
            m = m_new
        o = acc[:dv] / acc[dv:dv + 1]
        o_ref[0, qi * tq:(qi + 1) * tq, :] = o.T.astype(o_ref.dtype)


def _sb_attn_kernel(q_ref, k_ref, v_ref, u_ref, o_ref, vT_ref, *, scale, S):
    tq, tk = ATTN_TQ, ATTN_TK
    c = scale * LOG2E
    key, qry = _tile_masks()
    strict = key < qry
    blocks = _causal_tiles(S)
    order = [(qi, ki) for qi, kis in blocks for ki in reversed(kis)]
    q2, k2 = q_ref.at[0], k_ref.at[0]

    def v_tile(ki):
        return v_ref[0, ki * tk:(ki + 1) * tk, :].astype(F32)

    def gate(t, z_raw):
        qi, ki = t
        z = z_raw * c
        soft = jnp.log2(1.0 + jnp.exp2(-jnp.abs(z)))
        nlk = jnp.maximum(z, 0.0) + soft
        if ki == qi:
            nlk = jnp.where(strict, nlk, 0.0)
        hi = nlk.astype(BF16)
        lo = (nlk - hi.astype(F32)).astype(BF16)
        scan = jnp.dot(u_ref[...], jnp.concatenate([hi, lo], axis=0),
                       preferred_element_type=F32)
        return z, scan

    issue = _score_issuer(k2, q2, functools.partial(_transpose_v_tile, vT_ref, v_tile), len(blocks))
    scored = [issue(t) for t in order[:SCORE_LOOKAHEAD]]
    gated = []
    n_scored = [SCORE_LOOKAHEAD]
    n_gated = [0]

    def advance():
        if n_scored[0] < len(order):
            scored.append(issue(order[n_scored[0]]))
            n_scored[0] += 1
        if n_gated[0] < len(order):
            gated.append(gate(order[n_gated[0]], scored.pop(0)))
            n_gated[0] += 1

    for _ in range(SCAN_LOOKAHEAD):
        advance()
    for qi, kis in blocks:
        issue.prepare_ahead(qi)
        suf = acc = None
        for ki in reversed(kis):
            z, scan = gated.pop(0)
            advance()
            x = z + scan
            if suf is not None:
                x = x + suf
            a = jnp.exp2(x)
            if ki == qi:
                a = jnp.where(strict, a, 0.0)
            pv = jnp.dot(vT_ref[:, ki * tk:(ki + 1) * tk], a.astype(BF16),
                         preferred_element_type=F32)
            acc = pv if acc is None else acc + pv
            col = scan[0:1, :]
            suf = col if suf is None else suf + col
        o_ref[0, qi * tq:(qi + 1) * tq, :] = acc.T.astype(o_ref.dtype)


def _sb_attention(qkv, B, S):
    tk = ATTN_TK
    upper = (jnp.arange(tk)[None, :] >= jnp.arange(tk)[:, None])
    neg_u2 = jnp.tile(jnp.where(upper, -1.0, 0.0), (1, 2)).astype(BF16)
    H = N_HEADS
    return pl.pallas_call(
        functools.partial(_sb_attn_kernel, scale=1.0 / math.sqrt(HEAD_DIM), S=S),
        grid=(B, H),
        in_specs=[pl.BlockSpec((1, S, HEAD_DIM), lambda b, h: (b, 0, h)),
                  pl.BlockSpec((1, S, HEAD_DIM), lambda b, h: (b, 0, H + h)),
                  pl.BlockSpec((1, S, HEAD_DIM), lambda b, h: (b, 0, 2 * H + h)),
                  pl.BlockSpec((tk, 2 * tk), lambda b, h: (0, 0))],
        out_specs=pl.BlockSpec((1, S, HEAD_DIM), lambda b, h: (b, 0, h)),
        out_shape=jax.ShapeDtypeStruct((B, S, ATTN_WIDTH), BF16),
        scratch_shapes=[pltpu.VMEM((HEAD_DIM, S), BF16)],
        compiler_params=_params(2), name="sb_attention")(qkv, qkv, qkv, neg_u2)


def _fox_attn_kernel(q_ref, k_ref, v_ref, cfc_ref, cfr_ref, o_ref, vT_ref, slab_ref, *, scale, S):
    h = pl.program_id(1)

    def v_tile(ki):
        return v_ref[0, ki * ATTN_TK:(ki + 1) * ATTN_TK, :].astype(F32)

    def prep(c):
        rows = slice(c * ATTN_TK, (c + 1) * ATTN_TK)
        blk = cfc_ref[0, rows, :]
        lane = lax.broadcasted_iota(jnp.int32, blk.shape, 1)
        col = jnp.sum(jnp.where(lane == h, blk, 0.0), axis=1, keepdims=True) * LOG2E
        slab_ref[rows, :] = jnp.broadcast_to(col, blk.shape)
        _transpose_v_tile(vT_ref, v_tile, c)

    def bias(ki, qi):
        cf_q = cfr_ref[0, 0, :, qi * ATTN_TQ:(qi + 1) * ATTN_TQ] * LOG2E
        cf_k = slab_ref[ki * ATTN_TK:(ki + 1) * ATTN_TK, :]
        return cf_q - jnp.concatenate([cf_k] * (ATTN_TQ // LANES), axis=1)

    _softmax_sweep(q_ref.at[0], k_ref.at[0], vT_ref, o_ref, prep=prep, scale=scale, bias=bias, S=S)


def _fox_attention(qkv, cf_col, cf_row, B, S):
    H = N_HEADS
    return pl.pallas_call(
        functools.partial(_fox_attn_kernel, scale=1.0 / math.sqrt(HEAD_DIM), S=S),
        grid=(B, H),
        in_specs=[pl.BlockSpec((1, S, HEAD_DIM), lambda b, h: (b, 0, h)),
                  pl.BlockSpec((1, S, HEAD_DIM), lambda b, h: (b, 0, H + h)),
                  pl.BlockSpec((1, S, HEAD_DIM), lambda b, h: (b, 0, 2 * H + h)),
                  pl.BlockSpec((1, S, LANES), lambda b, h: (b, 0, 0)),
                  pl.BlockSpec((1, 1, 1, S), lambda b, h: (b, h, 0, 0))],
        out_specs=pl.BlockSpec((1, S, HEAD_DIM), lambda b, h: (b, 0, h)),
        out_shape=jax.ShapeDtypeStruct((B, S, ATTN_WIDTH), BF16),
        scratch_shapes=[pltpu.VMEM((HEAD_DIM + ONES_ROWS, S), BF16), pltpu.VMEM((S, LANES), F32)],
        compiler_params=_params(2), name="fox_attention")(qkv, qkv, qkv, cf_col, cf_row)


def _mla_attn_kernel(qf_ref, kn_ref, v_ref, kr_ref, cos_ref, sin_ref, qg_ref, kg_ref, o_ref,
                     q_scr, k_scr, vT_scr, *, scale, S):
    qg = qg_ref[...]
    kg = kg_ref[...]
    pad = jnp.zeros((ATTN_TK, MLA_QK_PAD - MLA_QK), F32)

    def normed(nope, roped, g):
        full = jnp.concatenate([nope, roped, pad], axis=1)
        ms = jnp.sum(full * full, axis=-1, keepdims=True) * (1.0 / MLA_QK)
        return (full * lax.rsqrt(ms + EPS) * g).astype(BF16)

    def v_tile(ki):
        return v_ref[0, ki * ATTN_TK:(ki + 1) * ATTN_TK, :]

    def prep(c):
        rows = slice(c * ATTN_TK, (c + 1) * ATTN_TK)
        qf = qf_ref[0, rows, :]
        q_rope = _rope(qf[:, MLA_NOPE:MLA_QK], cos_ref[0, rows, :], sin_ref[0, rows, :])
        q_scr[rows, :] = normed(qf[:, :MLA_NOPE], q_rope, qg)
        k_scr[rows, :] = normed(kn_ref[0, rows, :], kr_ref[0, rows, :], kg)
        _transpose_v_tile(vT_scr, v_tile, c)

    _softmax_sweep(q_scr, k_scr, vT_scr, o_ref, prep=prep, scale=scale, bias=lambda ki, qi: None, S=S)


def _mla_attention(q_full, kv, k_rope, cos2, sin2, q_gain, k_gain, B, S):
    H = N_HEADS
    return pl.pallas_call(
        functools.partial(_mla_attn_kernel, scale=1.0 / math.sqrt(MLA_QK), S=S),
        grid=(B, H),
        in_specs=[pl.BlockSpec((1, S, MLA_QK_PAD), lambda b, h: (b, 0, h)),
                  pl.BlockSpec((1, S, MLA_NOPE), lambda b, h: (b, 0, h)),
                  pl.BlockSpec((1, S, MLA_V), lambda b, h: (b, 0, H + h)),
                  pl.BlockSpec((1, S, MLA_ROPE), lambda b, h: (b, 0, 0)),
                  pl.BlockSpec((1, S, MLA_ROPE), lambda b, h: (b, 0, 0)),
                  pl.BlockSpec((1, S, MLA_ROPE), lambda b, h: (b, 0, 0)),
                  pl.BlockSpec((1, MLA_QK_PAD), lambda b, h: (0, 0)),
                  pl.BlockSpec((1, MLA_QK_PAD), lambda b, h: (0, 0))],
        out_specs=pl.BlockSpec((1, S, MLA_V), lambda b, h: (b, 0, h)),
        out_shape=jax.ShapeDtypeStruct((B, S, H * MLA_V), BF16),
        scratch_shapes=[pltpu.VMEM((S, MLA_QK_PAD), BF16), pltpu.VMEM((S, MLA_QK_PAD), BF16),
                        pltpu.VMEM((MLA_V + ONES_ROWS, S), BF16)],
        compiler_params=_params(2), name="mla_attention")(
            q_full, kv, kv, k_rope, cos2, sin2, q_gain, k_gain)


def _fox_gate_kernel(x_ref, g_ref, w_ref, b_ref, tri_ref, col_ref, row_ref, carry_ref):
    @pl.when(pl.program_id(1) == 0)
    def _():
        carry_ref[...] = jnp.zeros(carry_ref.shape, F32)

    h = (_rms(x_ref[0]) * g_ref[...]).astype(BF16)
    f = jnp.dot(h, w_ref[...], preferred_element_type=F32) + b_ref[...]
    log_f = jnp.minimum(f, 0.0) - jnp.log1p(jnp.exp(-jnp.abs(f)))
    p0 = log_f.astype(BF16)
    r1 = log_f - p0.astype(F32)
    p1 = r1.astype(BF16)
    p2 = (r1 - p1.astype(F32)).astype(BF16)
    tri = tri_ref[...]
    cs = (jnp.dot(tri, p0, preferred_element_type=F32) + jnp.dot(tri, p1, preferred_element_type=F32)
          + jnp.dot(tri, p2, preferred_element_type=F32)) + carry_ref[...]
    col_ref[0] = cs
    row_ref[0] = cs.T
    carry_ref[...] = cs[cs.shape[0] - 1:, :]


def _fox_gate(x, gain, w_f, b_f, ts=512):
    B, S, K = x.shape
    tri = (jnp.arange(ts)[None, :] <= jnp.arange(ts)[:, None]).astype(BF16)
    return pl.pallas_call(
        _fox_gate_kernel, grid=(B, S // ts),
        in_specs=[pl.BlockSpec((1, ts, K), lambda b, i: (b, i, 0)),
                  pl.BlockSpec((1, K), lambda b, i: (0, 0)),
                  pl.BlockSpec((K, LANES), lambda b, i: (0, 0)),
                  pl.BlockSpec((1, LANES), lambda b, i: (0, 0)),
                  pl.BlockSpec((ts, ts), lambda b, i: (0, 0))],
        out_specs=[pl.BlockSpec((1, ts, LANES), lambda b, i: (b, i, 0)),
                   pl.BlockSpec((1, LANES, ts), lambda b, i: (b, 0, i))],
        out_shape=[jax.ShapeDtypeStruct((B, S, LANES), F32), jax.ShapeDtypeStruct((B, LANES, S), F32)],
        scratch_shapes=[pltpu.VMEM((1, LANES), F32)],
        compiler_params=_params(2), name="fox_gate")(x, gain, w_f, b_f, tri)


def _rope_table_kernel(pos_ref, inv_ref, cos_ref, sin_ref):
    ang = pos_ref[0].astype(F32) * inv_ref[...]
    cos_ref[0] = jnp.cos(ang)
    sin_ref[0] = jnp.sin(ang)


def _rope_tables(positions):
    B, S = positions.shape
    half = MLA_ROPE // 2
    inv_freq = ROPE_THETA ** (-jnp.arange(0, half, dtype=F32) * 2.0 / MLA_ROPE)
    inv2 = jnp.concatenate([inv_freq, inv_freq])[None, :]
    out = jax.ShapeDtypeStruct((B, S, MLA_ROPE), F32)
    return pl.pallas_call(
        _rope_table_kernel, grid=(B,),
        in_specs=[pl.BlockSpec((1, S, 1), lambda b: (b, 0, 0)),
                  pl.BlockSpec((1, MLA_ROPE), lambda b: (0, 0))],
        out_specs=[pl.BlockSpec((1, S, MLA_ROPE), lambda b: (b, 0, 0))] * 2,
        out_shape=[out, out], compiler_params=_params(1), name="rope_tables")(
            positions.reshape(B, S, 1), inv2)


def _mla_down_kernel(x_ref, g_ref, w_ref, qn_ref, kvn_ref, cos_ref, sin_ref, cq_ref, ckv_ref, kr_ref):
    h = (_rms(x_ref[0]) * g_ref[...]).astype(BF16)
    down = jnp.dot(h, w_ref[...], preferred_element_type=F32)
    kv0 = MLA_Q_RANK
    r0 = MLA_Q_RANK + MLA_KV_RANK
    cq_ref[0] = (_rms(down[:, :kv0]) * qn_ref[...]).astype(BF16)
    ckv_ref[0] = (_rms(down[:, kv0:r0]) * kvn_ref[...]).astype(BF16)
    kr_ref[0] = _rope(down[:, r0:r0 + MLA_ROPE], cos_ref[0], sin_ref[0])


def _mla_down(x, gain, w_pad, q_norm, kv_norm, cos2, sin2, tm=512):
    B, S, K = x.shape
    n_pad = w_pad.shape[1]
    row = lambda b, i: (b, i, 0)
    fixed = lambda b, i: (0, 0)
    return pl.pallas_call(
        _mla_down_kernel, grid=(B, S // tm),
        in_specs=[pl.BlockSpec((1, tm, K), row), pl.BlockSpec((1, K), fixed),
                  pl.BlockSpec((K, n_pad), fixed),
                  pl.BlockSpec((1, MLA_Q_RANK), fixed), pl.BlockSpec((1, MLA_KV_RANK), fixed),
                  pl.BlockSpec((1, tm, MLA_ROPE), row), pl.BlockSpec((1, tm, MLA_ROPE), row)],
        out_specs=[pl.BlockSpec((1, tm, MLA_Q_RANK), row), pl.BlockSpec((1, tm, MLA_KV_RANK), row),
                   pl.BlockSpec((1, tm, MLA_ROPE), row)],
        out_shape=[jax.ShapeDtypeStruct((B, S, MLA_Q_RANK), BF16),
                   jax.ShapeDtypeStruct((B, S, MLA_KV_RANK), BF16),
                   jax.ShapeDtypeStruct((B, S, MLA_ROPE), F32)],
        compiler_params=_params(2), name="mla_down")(x, gain, w_pad, q_norm, kv_norm, cos2, sin2)


def _out_proj(o, w_out, x, *, name):
    return _matmul(o, w_out.astype(BF16)[None], res=x, out_dtype=F32, tm=OUT_TM, tn=w_out.shape[1],
                   name=name)


def _sb_layer(x, gain, w_in, w_out, j):
    B, S, _ = x.shape
    qkv = _matmul(x, w_in, j, gain=gain, out_dtype=BF16, name="sb_qkv")
    o = _sb_attention(qkv, B, S)
    return _out_proj(o, w_out[j], x, name="sb_out")


def _fox_layer(x, gain, w_in, b_f, q_gain, k_gain, w_out, j):
    B, S, _ = x.shape
    w_f = jnp.pad(w_in[j, :, 3 * ATTN_WIDTH:], ((0, 0), (0, LANES - N_HEADS))).astype(BF16)
    b_pad = jnp.pad(b_f, (0, LANES - N_HEADS))[None, :]
    qkv_gain = jnp.concatenate([jnp.tile(q_gain, N_HEADS), jnp.tile(k_gain, N_HEADS),
                                jnp.ones((ATTN_WIDTH,), F32)])[None, :]
    qkv = _matmul(x, jnp.swapaxes(w_in, 1, 2), j, w_rows=True, n_cols=3 * ATTN_WIDTH, gain=gain,
                  epi=_head_norm_epi, epi_tiles=2 * ATTN_WIDTH // MM_TN, extras=(qkv_gain,),
                  out_dtype=BF16, name="fox_qkv")
    cf_col, cf_rowT = _fox_gate(x, gain, w_f, b_pad)
    cf_row = cf_rowT[:, :N_HEADS, :].reshape(B, N_HEADS, 1, S)
    o = _fox_attention(qkv, cf_col, cf_row, B, S)
    return _out_proj(o, w_out[j], x, name="fox_out")


def _mla_layer(x, positions, gain, w_in, q_norm, kv_norm, w_uq, w_ukv, q_gain, k_gain, w_out_all, j):
    B, S, _ = x.shape
    n_down = w_in.shape[1]
    n_pad = -(-n_down // LANES) * LANES
    w_down = jnp.pad(w_in, ((0, 0), (0, n_pad - n_down))).astype(BF16)
    w_q = jnp.pad(w_uq.reshape(MLA_Q_RANK, N_HEADS, MLA_QK),
                  ((0, 0), (0, 0), (0, MLA_QK_PAD - MLA_QK))).reshape(MLA_Q_RANK, -1).astype(BF16)
    w_kv = w_ukv.reshape(MLA_KV_RANK, N_HEADS, MLA_NOPE + MLA_V)
    w_kv = jnp.concatenate([w_kv[:, :, :MLA_NOPE].reshape(MLA_KV_RANK, -1),
                            w_kv[:, :, MLA_NOPE:].reshape(MLA_KV_RANK, -1)], axis=1).astype(BF16)
    qg = jnp.pad(q_gain, (0, MLA_QK_PAD - MLA_QK))[None, :]
    kg = jnp.pad(k_gain, (0, MLA_QK_PAD - MLA_QK))[None, :]

    cos2, sin2 = _rope_tables(positions)
    c_q, c_kv, k_rope = _mla_down(x, gain, w_down, q_norm[None, :], kv_norm[None, :], cos2, sin2)
    q_full = _matmul(c_q, w_q[None], out_dtype=F32, name="mla_uq")
    kv = _matmul(c_kv, w_kv[None], out_dtype=F32, name="mla_ukv")
    o = _mla_attention(q_full, kv, k_rope, cos2, sin2, qg, kg, B, S)
    return _out_proj(o, w_out_all[j], x, name="mla_out")


def _mlp(x, gain, w1, w2_bf16, i):
    a = _matmul(x, w1, i, gain=gain, epi=_relu2_epi, out_dtype=BF16, name="mlp_up")
    return _matmul_ksplit(a, w2_bf16, i, x, name="mlp_down")


def kernel(x, positions, mix_norm, mlp_norm, sb_w_in, sb_w_out, fox_w_in, fox_b_f, fox_q_gain,
           fox_k_gain, fox_w_out, mla_w_in, mla_q_norm, mla_kv_norm, mla_w_uq, mla_w_ukv,
           mla_q_gain, mla_k_gain, mla_w_out, mlp_w1, mlp_w2):
    depth = mix_norm.shape[0]
    w2_bf16 = mlp_w2.astype(BF16)
    for i in range(depth):
        kind, j = i % N_MIXERS, i // N_MIXERS
        gain = mix_norm[i][None, :]
        if kind == 0:
            x = _sb_layer(x, gain, sb_w_in, sb_w_out, j)
        elif kind == 1:
            x = _fox_layer(x, gain, fox_w_in, fox_b_f[j], fox_q_gain[j], fox_k_gain[j], fox_w_out, j)
        else:
            x = _mla_layer(x, positions, gain, mla_w_in[j], mla_q_norm[j], mla_kv_norm[j], mla_w_uq[j],
                           mla_w_ukv[j], mla_q_gain[j], mla_k_gain[j], mla_w_out, j)
        x = _mlp(x, mlp_norm[i][None, :], mlp_w1, w2_bf16, i)
    return x
```

```python
import functools
import math

import jax
import jax.numpy as jnp
from jax import lax
from jax.experimental import pallas as pl
from jax.experimental.pallas import tpu as pltpu

F32 = jnp.float32
BF16 = jnp.bfloat16

D_MODEL = 2048
HEAD_DIM = 128
N_HEADS = D_MODEL // HEAD_DIM
ATTN_WIDTH = N_HEADS * HEAD_DIM
D_FF = 4 * D_MODEL
MLA_Q_RANK = 3 * D_MODEL // 8
MLA_KV_RANK = D_MODEL // 4
MLA_NOPE = 128
MLA_ROPE = 64
MLA_V = 128
MLA_QK = MLA_NOPE + MLA_ROPE
MLA_QK_PAD = 256
ROPE_THETA = 10000.0
EPS = 1e-6
N_MIXERS = 3
LOG2E = math.log2(math.e)

LANES = 128
ATTN_TQ = 256
ATTN_TK = 256
SCORE_LOOKAHEAD = 5
ONES_ROWS = 16
PREP_AHEAD = 4
SCAN_LOOKAHEAD = 2
MM_TM = 1024
MM_TN = 1024
MLP_TK = 2048
OUT_TM = 512
OUT_ROW_SPLIT = 2
EPI_ROW_SPLIT = 4
VMEM_LIMIT = 48 * 1024 * 1024

_NT = (((1,), (1,)), ((), ()))


def _params(n_grid):
    return pltpu.CompilerParams(dimension_semantics=("arbitrary",) * n_grid,
                                vmem_limit_bytes=VMEM_LIMIT)


def _rms(x):
    ms = jnp.mean(x * x, axis=-1, keepdims=True)
    return x * lax.rsqrt(ms + EPS)


def _rope_pair(pair, cos_sin):
    prod = pair * cos_sin
    return prod + pltpu.roll(prod, MLA_ROPE, axis=1)


def _rot_columns(w_rope):
    half = w_rope.shape[-1] // 2
    return jnp.concatenate([-w_rope[..., half:], w_rope[..., :half]], axis=-1)


def _mm_kernel(*refs, norm, w_rows, epi, epi_tiles, n_extra, n_row_extra, has_res, emit_norm, row_split):
    it = iter(refs)
    x_ref = next(it)
    g_ref = next(it) if norm else None
    w_ref = next(it)
    extra = [next(it) for _ in range(n_extra)]
    row_extra = [next(it) for _ in range(n_row_extra)]
    res_ref = next(it) if has_res else None
    gn_ref = next(it) if emit_norm else None
    o_ref = next(it)
    hn_ref = next(it) if emit_norm else None
    h_ref = next(it) if norm else None

    if norm:
        @pl.when(pl.program_id(2) == 0)
        def _():
            h_ref[...] = (_rms(x_ref[0]) * g_ref[...]).astype(BF16)

    w = w_ref[0].astype(BF16)
    sub = o_ref.shape[1] // row_split
    active = None if epi_tiles is None else pl.program_id(2) < epi_tiles

    def product(r):
        rows = slice(r * sub, (r + 1) * sub)
        h = h_ref[rows, :] if norm else x_ref[0, rows, :]
        if w_rows:
            return lax.dot_general(h, w, _NT, preferred_element_type=F32)
        return jnp.dot(h, w, preferred_element_type=F32)

    pending = [product(0)]
    for r in range(row_split):
        if r + 1 < row_split:
            pending.append(product(r + 1))
        rows = slice(r * sub, (r + 1) * sub)
        acc = pending.pop(0)
        if has_res:
            acc = res_ref[0, rows, :] + acc
        if epi is not None:
            acc = epi(acc, *extra, *[e[0, rows, :] for e in row_extra], active=active)
        o_ref[0, rows, :] = acc.astype(o_ref.dtype)
        if emit_norm:
            hn_ref[0, rows, :] = (_rms(acc) * gn_ref[...]).astype(BF16)


def _matmul(x, w, layer=0, *, w_rows=False, n_cols=None, gain=None, epi=None, epi_tiles=None, extras=(),
            row_extras=(), res=None, next_gain=None, out_dtype, tm=MM_TM, tn=MM_TN, out_tn=None,
            row_split=1, name):
    B, S, K = x.shape
    N = w.shape[1 if w_rows else 2] if n_cols is None else n_cols
    out_tn = tn if out_tn is None else out_tn
    n_out = N // tn * out_tn
    norm = gain is not None
    emit_norm = next_gain is not None
    assert not emit_norm or tn == N
    grid = (B, S // tm, N // tn)
    tile = lambda b, i, j: (b, i, j)
    in_specs = [pl.BlockSpec((1, tm, K), lambda b, i, j: (b, i, 0))]
    args = [x]
    if norm:
        in_specs.append(pl.BlockSpec((1, K), lambda b, i, j: (0, 0)))
        args.append(gain)
    if w_rows:
        in_specs.append(pl.BlockSpec((1, tn, K), lambda b, i, j: (layer, j, 0)))
    else:
        in_specs.append(pl.BlockSpec((1, K, tn), lambda b, i, j: (layer, 0, j)))
    args.append(w)
    for e in extras:
        in_specs.append(pl.BlockSpec((1, out_tn), lambda b, i, j: (0, j)))
        args.append(e)
    for e in row_extras:
        in_specs.append(pl.BlockSpec((1, tm, e.shape[2]), lambda b, i, j: (b, i, 0)))
        args.append(e)
    if res is not None:
        in_specs.append(pl.BlockSpec((1, tm, tn), tile))
        args.append(res)
    out_specs = pl.BlockSpec((1, tm, out_tn), tile)
    out_shape = jax.ShapeDtypeStruct((B, S, n_out), out_dtype)
    if emit_norm:
        in_specs.append(pl.BlockSpec((1, N), lambda b, i, j: (0, 0)))
        args.append(next_gain)
        out_specs = [out_specs, pl.BlockSpec((1, tm, tn), tile)]
        out_shape = [out_shape, jax.ShapeDtypeStruct((B, S, N), BF16)]
    body = functools.partial(_mm_kernel, norm=norm, w_rows=w_rows, epi=epi, epi_tiles=epi_tiles,
                             n_extra=len(extras), n_row_extra=len(row_extras),
                             has_res=res is not None, emit_norm=emit_norm, row_split=row_split)
    return pl.pallas_call(
        body, grid=grid, in_specs=in_specs, out_specs=out_specs, out_shape=out_shape,
        scratch_shapes=[pltpu.VMEM((tm, K), BF16)] if norm else [],
        compiler_params=_params(3), name=name)(*args)


def _mm_ksplit_kernel(x_ref, w_ref, res_ref, o_ref):
    @pl.when(pl.program_id(3) == 0)
    def _():
        o_ref[0] = res_ref[0]

    o_ref[0] += jnp.dot(x_ref[0], w_ref[0].astype(BF16), preferred_element_type=F32)


def _matmul_ksplit(x, w, layer, res, *, tm=MM_TM, tn=MM_TN, tk=MLP_TK, name):
    B, S, K = x.shape
    N = w.shape[2]
    return pl.pallas_call(
        _mm_ksplit_kernel, grid=(B, S // tm, N // tn, K // tk),
        in_specs=[pl.BlockSpec((1, tm, tk), lambda b, i, j, k: (b, i, k)),
                  pl.BlockSpec((1, tk, tn), lambda b, i, j, k: (layer, k, j)),
                  pl.BlockSpec((1, tm, tn), lambda b, i, j, k: (b, i, j))],
        out_specs=pl.BlockSpec((1, tm, tn), lambda b, i, j, k: (b, i, j)),
        out_shape=jax.ShapeDtypeStruct((B, S, N), F32),
        compiler_params=_params(4), name=name)(x, w, res)


def _relu2_epi(acc, active=None):
    a = jnp.maximum(acc, 0.0)
    return a * a


def _head_norm_epi(acc, g_ref, active=None):
    g = g_ref[...]
    outs = []
    for c in range(acc.shape[1] // HEAD_DIM):
        sl = slice(c * HEAD_DIM, (c + 1) * HEAD_DIM)
        blk = acc[:, sl]
        inv = lax.rsqrt(jnp.mean(blk * blk, axis=-1, keepdims=True) + EPS)
        if active is not None:
            inv = jnp.where(active, inv, 1.0)
        outs.append(blk * inv * g[:, sl])
    return jnp.concatenate(outs, axis=1)


def _mla_head_norm(nope, roped_twice, g):
    sq = nope * nope + 0.5 * (roped_twice * roped_twice)
    inv = lax.rsqrt(jnp.sum(sq, axis=-1, keepdims=True) * (1.0 / MLA_QK) + EPS)
    return jnp.concatenate([nope, roped_twice], axis=1) * inv * g


def _mla_q_epi(acc, g_ref, cos_sin, active=None):
    g = g_ref[...]
    outs = []
    for c in range(acc.shape[1] // MLA_QK_PAD):
        head = acc[:, c * MLA_QK_PAD:(c + 1) * MLA_QK_PAD]
        roped = _rope_pair(head[:, MLA_NOPE:], cos_sin)
        outs.append(_mla_head_norm(head[:, :MLA_NOPE], roped, g[:, c * MLA_QK_PAD:(c + 1) * MLA_QK_PAD]))
    return jnp.concatenate(outs, axis=1)


def _mla_k_epi(acc, g_ref, k_rope, active=None):
    g = g_ref[...]
    outs = []
    for c in range(acc.shape[1] // MLA_NOPE):
        outs.append(_mla_head_norm(acc[:, c * MLA_NOPE:(c + 1) * MLA_NOPE], k_rope,
                                   g[:, c * MLA_QK_PAD:(c + 1) * MLA_QK_PAD]))
    return jnp.concatenate(outs, axis=1)


def _causal_tiles(S):
    return [(qi, list(range(qi + 1))) for qi in range(S // ATTN_TQ)]


def _tile_masks():
    key = lax.broadcasted_iota(jnp.int32, (ATTN_TK, ATTN_TQ), 0)
    qry = lax.broadcasted_iota(jnp.int32, (ATTN_TK, ATTN_TQ), 1)
    return key, qry


def _raw_scores(k_ref, q_ref, qi, ki):
    return lax.dot_general(k_ref[ki * ATTN_TK:(ki + 1) * ATTN_TK, :],
                           q_ref[qi * ATTN_TQ:(qi + 1) * ATTN_TQ, :], _NT,
                           preferred_element_type=F32)


def _transpose_v_tile(vT_ref, v_tile, ki):
    vt = v_tile(ki).T.astype(BF16)
    vT_ref[0:vt.shape[0], ki * ATTN_TK:(ki + 1) * ATTN_TK] = vt


class _score_issuer:
    def __init__(self, k_ref, q_ref, prep, n_blocks):
        self.k_ref, self.q_ref, self.prep, self.n_blocks = k_ref, q_ref, prep, n_blocks
        self.ready = 0
        for _ in range(min(PREP_AHEAD, n_blocks)):
            self._prepare_next()

    def _prepare_next(self):
        self.prep(self.ready)
        self.ready += 1

    def prepare_ahead(self, qi):
        if qi + PREP_AHEAD < self.n_blocks:
            self._prepare_next()

    def __call__(self, tile):
        qi, ki = tile
        assert max(qi, ki) < self.ready, "score tile issued before its rows were prepared"
        return _raw_scores(self.k_ref, self.q_ref, qi, ki)


def _softmax_sweep(q_ref, k_ref, vT_ref, o_ref, *, prep, scale, bias, S):
    tq, tk = ATTN_TQ, ATTN_TK
    c = scale * LOG2E
    key, qry = _tile_masks()
    causal = key <= qry
    blocks = _causal_tiles(S)
    order = [(qi, ki) for qi, kis in blocks for ki in kis]
    dv = vT_ref.shape[0] - ONES_ROWS
    vT_ref[dv:, :] = jnp.ones((ONES_ROWS, S), BF16)
    issue = _score_issuer(k_ref, q_ref, prep, len(blocks))
    pending = [issue(t) for t in order[:SCORE_LOOKAHEAD]]
    issued = SCORE_LOOKAHEAD
    for qi, kis in blocks:
        issue.prepare_ahead(qi)
        m = acc = None
        for ki in kis:
            s_raw = pending.pop(0)
            if issued < len(order):
                pending.append(issue(order[issued]))
                issued += 1
            s = s_raw * c
            b = bias(ki, qi)
            if b is not None:
                s = s + b
            if ki == qi:
                s = jnp.where(causal, s, -jnp.inf)
            s_max = jnp.max(s, axis=0, keepdims=True)
            m_new = s_max if m is None else jnp.maximum(m, s_max)
            p = jnp.exp2(s - m_new)
            pv = jnp.dot(vT_ref[:, ki * tk:(ki + 1) * tk], p.astype(BF16),
                         preferred_element_type=F32)
            acc = pv if m is None else jnp.exp2(m - m_new) * acc + pv
            m = m_new
        o = acc[:dv] / acc[dv:dv + 1]
        o_ref[0, qi * tq:(qi + 1) * tq, :] = o.T.astype(o_ref.dtype)


def _sb_attn_kernel(q_ref, k_ref, v_ref, u_ref, o_ref, vT_ref, *, scale, S):
    tq, tk = ATTN_TQ, ATTN_TK
    c = scale * LOG2E
    key, qry = _tile_masks()
    strict = key < qry
    blocks = _causal_tiles(S)
    order = [(qi, ki) for qi, kis in blocks for ki in reversed(kis)]
    q2, k2 = q_ref.at[0], k_ref.at[0]

    def v_tile(ki):
        return v_ref[0, ki * tk:(ki + 1) * tk, :].astype(F32)

    def gate(t, z_raw):
        qi, ki = t
        z = z_raw * c
        soft = jnp.log2(1.0 + jnp.exp2(-jnp.abs(z)))
        nlk = jnp.maximum(z, 0.0) + soft
        if ki == qi:
            nlk = jnp.where(strict, nlk, 0.0)
        hi = nlk.astype(BF16)
        lo = (nlk - hi.astype(F32)).astype(BF16)
        scan = jnp.dot(u_ref[...], jnp.concatenate([hi, lo], axis=0),
                       preferred_element_type=F32)
        return z, scan

    issue = _score_issuer(k2, q2, functools.partial(_transpose_v_tile, vT_ref, v_tile), len(blocks))
    scored = [issue(t) for t in order[:SCORE_LOOKAHEAD]]
    gated = []
    n_scored = [SCORE_LOOKAHEAD]
    n_gated = [0]

    def advance():
        if n_scored[0] < len(order):
            scored.append(issue(order[n_scored[0]]))
            n_scored[0] += 1
        if n_gated[0] < len(order):
            gated.append(gate(order[n_gated[0]], scored.pop(0)))
            n_gated[0] += 1

    for _ in range(SCAN_LOOKAHEAD):
        advance()
    for qi, kis in blocks:
        issue.prepare_ahead(qi)
        suf = acc = None
        for ki in reversed(kis):
            z, scan = gated.pop(0)
            advance()
            x = z + scan
            if suf is not None:
                x = x + suf
            a = jnp.exp2(x)
            if ki == qi:
                a = jnp.where(strict, a, 0.0)
            pv = jnp.dot(vT_ref[:, ki * tk:(ki + 1) * tk], a.astype(BF16),
                         preferred_element_type=F32)
            acc = pv if acc is None else acc + pv
            col = scan[0:1, :]
            suf = col if suf is None else suf + col
        o_ref[0, qi * tq:(qi + 1) * tq, :] = acc.T.astype(o_ref.dtype)


def _sb_attention(qkv, B, S):
    tk = ATTN_TK
    upper = (jnp.arange(tk)[None, :] >= jnp.arange(tk)[:, None])
    neg_u2 = jnp.tile(jnp.where(upper, -1.0, 0.0), (1, 2)).astype(BF16)
    H = N_HEADS
    return pl.pallas_call(
        functools.partial(_sb_attn_kernel, scale=1.0 / math.sqrt(HEAD_DIM), S=S),
        grid=(B, H),
        in_specs=[pl.BlockSpec((1, S, HEAD_DIM), lambda b, h: (b, 0, h)),
                  pl.BlockSpec((1, S, HEAD_DIM), lambda b, h: (b, 0, H + h)),
                  pl.BlockSpec((1, S, HEAD_DIM), lambda b, h: (b, 0, 2 * H + h)),
                  pl.BlockSpec((tk, 2 * tk), lambda b, h: (0, 0))],
        out_specs=pl.BlockSpec((1, S, HEAD_DIM), lambda b, h: (b, 0, h)),
        out_shape=jax.ShapeDtypeStruct((B, S, ATTN_WIDTH), BF16),
        scratch_shapes=[pltpu.VMEM((HEAD_DIM, S), BF16)],
        compiler_params=_params(2), name="sb_attention")(qkv, qkv, qkv, neg_u2)


def _fox_attn_kernel(q_ref, k_ref, v_ref, cfc_ref, cfr_ref, o_ref, vT_ref, slab_ref, *, scale, S):
    h = pl.program_id(1)

    def v_tile(ki):
        return v_ref[0, ki * ATTN_TK:(ki + 1) * ATTN_TK, :].astype(F32)

    def prep(c):
        rows = slice(c * ATTN_TK, (c + 1) * ATTN_TK)
        blk = cfc_ref[0, rows, :]
        lane = lax.broadcasted_iota(jnp.int32, blk.shape, 1)
        col = jnp.sum(jnp.where(lane == h, blk, 0.0), axis=1, keepdims=True) * LOG2E
        slab_ref[rows, :] = jnp.broadcast_to(col, blk.shape)
        _transpose_v_tile(vT_ref, v_tile, c)

    def bias(ki, qi):
        cf_q = cfr_ref[0, 0, :, qi * ATTN_TQ:(qi + 1) * ATTN_TQ] * LOG2E
        cf_k = slab_ref[ki * ATTN_TK:(ki + 1) * ATTN_TK, :]
        return cf_q - jnp.concatenate([cf_k] * (ATTN_TQ // LANES), axis=1)

    _softmax_sweep(q_ref.at[0], k_ref.at[0], vT_ref, o_ref, prep=prep, scale=scale, bias=bias, S=S)


def _fox_attention(qkv, cf_col, cf_row, B, S):
    H = N_HEADS
    return pl.pallas_call(
        functools.partial(_fox_attn_kernel, scale=1.0 / math.sqrt(HEAD_DIM), S=S),
        grid=(B, H),
        in_specs=[pl.BlockSpec((1, S, HEAD_DIM), lambda b, h: (b, 0, h)),
                  pl.BlockSpec((1, S, HEAD_DIM), lambda b, h: (b, 0, H + h)),
                  pl.BlockSpec((1, S, HEAD_DIM), lambda b, h: (b, 0, 2 * H + h)),
                  pl.BlockSpec((1, S, LANES), lambda b, h: (b, 0, 0)),
                  pl.BlockSpec((1, 1, 1, S), lambda b, h: (b, h, 0, 0))],
        out_specs=pl.BlockSpec((1, S, HEAD_DIM), lambda b, h: (b, 0, h)),
        out_shape=jax.ShapeDtypeStruct((B, S, ATTN_WIDTH), BF16),
        scratch_shapes=[pltpu.VMEM((HEAD_DIM + ONES_ROWS, S), BF16), pltpu.VMEM((S, LANES), F32)],
        compiler_params=_params(2), name="fox_attention")(qkv, qkv, qkv, cf_col, cf_row)


def _mla_attn_kernel(q_ref, k_ref, v_ref, o_ref, vT_scr, *, scale, S):
    def v_tile(ki):
        return v_ref[0, ki * ATTN_TK:(ki + 1) * ATTN_TK, :].astype(F32)

    _softmax_sweep(q_ref.at[0], k_ref.at[0], vT_scr, o_ref,
                   prep=functools.partial(_transpose_v_tile, vT_scr, v_tile),
                   scale=scale, bias=lambda ki, qi: None, S=S)


def _mla_attention(q, k, v, B, S):
    H = N_HEADS
    return pl.pallas_call(
        functools.partial(_mla_attn_kernel, scale=1.0 / math.sqrt(MLA_QK), S=S),
        grid=(B, H),
        in_specs=[pl.BlockSpec((1, S, MLA_QK_PAD), lambda b, h: (b, 0, h)),
                  pl.BlockSpec((1, S, MLA_QK_PAD), lambda b, h: (b, 0, h)),
                  pl.BlockSpec((1, S, MLA_V), lambda b, h: (b, 0, h))],
        out_specs=pl.BlockSpec((1, S, MLA_V), lambda b, h: (b, 0, h)),
        out_shape=jax.ShapeDtypeStruct((B, S, H * MLA_V), BF16),
        scratch_shapes=[pltpu.VMEM((MLA_V + ONES_ROWS, S), BF16)],
        compiler_params=_params(2), name="mla_attention")(q, k, v)


def _fox_gate_kernel(x_ref, g_ref, w_ref, b_ref, tri_ref, col_ref, row_ref, carry_ref):
    @pl.when(pl.program_id(1) == 0)
    def _():
        carry_ref[...] = jnp.zeros(carry_ref.shape, F32)

    h = (_rms(x_ref[0]) * g_ref[...]).astype(BF16)
    f = jnp.dot(h, w_ref[...], preferred_element_type=F32) + b_ref[...]
    log_f = jnp.minimum(f, 0.0) - jnp.log1p(jnp.exp(-jnp.abs(f)))
    p0 = log_f.astype(BF16)
    r1 = log_f - p0.astype(F32)
    p1 = r1.astype(BF16)
    p2 = (r1 - p1.astype(F32)).astype(BF16)
    tri = tri_ref[...]
    cs = (jnp.dot(tri, p0, preferred_element_type=F32) + jnp.dot(tri, p1, preferred_element_type=F32)
          + jnp.dot(tri, p2, preferred_element_type=F32)) + carry_ref[...]
    col_ref[0] = cs
    row_ref[0] = cs.T
    carry_ref[...] = cs[cs.shape[0] - 1:, :]


def _fox_gate(x, gain, w_f, b_f, ts=512):
    B, S, K = x.shape
    tri = (jnp.arange(ts)[None, :] <= jnp.arange(ts)[:, None]).astype(BF16)
    return pl.pallas_call(
        _fox_gate_kernel, grid=(B, S // ts),
        in_specs=[pl.BlockSpec((1, ts, K), lambda b, i: (b, i, 0)),
                  pl.BlockSpec((1, K), lambda b, i: (0, 0)),
                  pl.BlockSpec((K, LANES), lambda b, i: (0, 0)),
                  pl.BlockSpec((1, LANES), lambda b, i: (0, 0)),
                  pl.BlockSpec((ts, ts), lambda b, i: (0, 0))],
        out_specs=[pl.BlockSpec((1, ts, LANES), lambda b, i: (b, i, 0)),
                   pl.BlockSpec((1, LANES, ts), lambda b, i: (b, 0, i))],
        out_shape=[jax.ShapeDtypeStruct((B, S, LANES), F32), jax.ShapeDtypeStruct((B, LANES, S), F32)],
        scratch_shapes=[pltpu.VMEM((1, LANES), F32)],
        compiler_params=_params(2), name="fox_gate")(x, gain, w_f, b_f, tri)


def _rope_table_kernel(pos_ref, inv_ref, cos_sin_ref):
    ang = pos_ref[0].astype(F32) * inv_ref[...]
    cos_sin_ref[0] = jnp.concatenate([jnp.cos(ang), jnp.sin(ang)], axis=1)


def _rope_table(positions):
    B, S = positions.shape
    half = MLA_ROPE // 2
    inv_freq = ROPE_THETA ** (-jnp.arange(0, half, dtype=F32) * 2.0 / MLA_ROPE)
    inv2 = jnp.concatenate([inv_freq, inv_freq])[None, :]
    return pl.pallas_call(
        _rope_table_kernel, grid=(B,),
        in_specs=[pl.BlockSpec((1, S, 1), lambda b: (b, 0, 0)),
                  pl.BlockSpec((1, MLA_ROPE), lambda b: (0, 0))],
        out_specs=pl.BlockSpec((1, S, 2 * MLA_ROPE), lambda b: (b, 0, 0)),
        out_shape=jax.ShapeDtypeStruct((B, S, 2 * MLA_ROPE), F32),
        compiler_params=_params(1), name="rope_table")(positions.reshape(B, S, 1), inv2)


def _mla_down_kernel(x_ref, g_ref, w_ref, qn_ref, kvn_ref, cos_sin_ref, cq_ref, ckv_ref, kr_ref):
    h = (_rms(x_ref[0]) * g_ref[...]).astype(BF16)
    down = jnp.dot(h, w_ref[...], preferred_element_type=F32)
    kv0 = MLA_Q_RANK
    r0 = MLA_Q_RANK + MLA_KV_RANK
    cq_ref[0] = (_rms(down[:, :kv0]) * qn_ref[...]).astype(BF16)
    ckv_ref[0] = (_rms(down[:, kv0:r0]) * kvn_ref[...]).astype(BF16)
    kr_ref[0] = _rope_pair(down[:, r0:r0 + 2 * MLA_ROPE], cos_sin_ref[0])


def _mla_down(x, gain, w_down, q_norm, kv_norm, cos_sin, tm=512):
    B, S, K = x.shape
    n_down = w_down.shape[1]
    row = lambda b, i: (b, i, 0)
    fixed = lambda b, i: (0, 0)
    return pl.pallas_call(
        _mla_down_kernel, grid=(B, S // tm),
        in_specs=[pl.BlockSpec((1, tm, K), row), pl.BlockSpec((1, K), fixed),
                  pl.BlockSpec((K, n_down), fixed),
                  pl.BlockSpec((1, MLA_Q_RANK), fixed), pl.BlockSpec((1, MLA_KV_RANK), fixed),
                  pl.BlockSpec((1, tm, 2 * MLA_ROPE), row)],
        out_specs=[pl.BlockSpec((1, tm, MLA_Q_RANK), row), pl.BlockSpec((1, tm, MLA_KV_RANK), row),
                   pl.BlockSpec((1, tm, 2 * MLA_ROPE), row)],
        out_shape=[jax.ShapeDtypeStruct((B, S, MLA_Q_RANK), BF16),
                   jax.ShapeDtypeStruct((B, S, MLA_KV_RANK), BF16),
                   jax.ShapeDtypeStruct((B, S, 2 * MLA_ROPE), F32)],
        compiler_params=_params(2), name="mla_down")(x, gain, w_down, q_norm, kv_norm, cos_sin)


def _out_proj(o, w_out, x, mlp_gain, *, name):
    return _matmul(o, w_out.astype(BF16)[None], res=x, next_gain=mlp_gain, out_dtype=F32, tm=OUT_TM,
                   tn=w_out.shape[1], row_split=OUT_ROW_SPLIT, name=name)


def _sb_layer(x, gain, mlp_gain, w_in, w_out, j):
    B, S, _ = x.shape
    qkv = _matmul(x, w_in, j, gain=gain, out_dtype=BF16, name="sb_qkv")
    o = _sb_attention(qkv, B, S)
    return _out_proj(o, w_out[j], x, mlp_gain, name="sb_out")


def _fox_layer(x, gain, mlp_gain, w_in, b_f, q_gain, k_gain, w_out, j):
    B, S, _ = x.shape
    w_f = jnp.pad(w_in[j, :, 3 * ATTN_WIDTH:], ((0, 0), (0, LANES - N_HEADS))).astype(BF16)
    b_pad = jnp.pad(b_f, (0, LANES - N_HEADS))[None, :]
    qkv_gain = jnp.concatenate([jnp.tile(q_gain, N_HEADS), jnp.tile(k_gain, N_HEADS),
                                jnp.ones((ATTN_WIDTH,), F32)])[None, :]
    qkv = _matmul(x, jnp.swapaxes(w_in, 1, 2), j, w_rows=True, n_cols=3 * ATTN_WIDTH, gain=gain,
                  epi=_head_norm_epi, epi_tiles=2 * ATTN_WIDTH // MM_TN, extras=(qkv_gain,),
                  out_dtype=BF16, row_split=OUT_ROW_SPLIT, name="fox_qkv")
    cf_col, cf_rowT = _fox_gate(x, gain, w_f, b_pad)
    cf_row = cf_rowT[:, :N_HEADS, :].reshape(B, N_HEADS, 1, S)
    o = _fox_attention(qkv, cf_col, cf_row, B, S)
    return _out_proj(o, w_out[j], x, mlp_gain, name="fox_out")


def _mla_layer(x, positions, gain, mlp_gain, w_in, q_norm, kv_norm, w_uq, w_ukv, q_gain, k_gain,
               w_out_all, j):
    B, S, _ = x.shape
    r0 = MLA_Q_RANK + MLA_KV_RANK
    w_down = jnp.concatenate([w_in, _rot_columns(w_in[:, r0:])], axis=1).astype(BF16)
    w_q = w_uq.reshape(MLA_Q_RANK, N_HEADS, MLA_QK)
    w_q = jnp.concatenate([w_q, _rot_columns(w_q[:, :, MLA_NOPE:])], axis=2)
    w_q = w_q.reshape(MLA_Q_RANK, -1).astype(BF16)
    w_kv = w_ukv.reshape(MLA_KV_RANK, N_HEADS, MLA_NOPE + MLA_V)
    w_k = w_kv[:, :, :MLA_NOPE].reshape(MLA_KV_RANK, -1).astype(BF16)
    w_v = w_kv[:, :, MLA_NOPE:].reshape(MLA_KV_RANK, -1).astype(BF16)
    qg = jnp.tile(jnp.pad(q_gain, (0, MLA_QK_PAD - MLA_QK)), N_HEADS)[None, :]
    kg = jnp.tile(jnp.pad(k_gain, (0, MLA_QK_PAD - MLA_QK)), N_HEADS)[None, :]

    cos_sin = _rope_table(positions)
    c_q, c_kv, k_rope = _mla_down(x, gain, w_down, q_norm[None, :], kv_norm[None, :], cos_sin)
    q = _matmul(c_q, w_q[None], epi=_mla_q_epi, extras=(qg,), row_extras=(cos_sin,), out_dtype=BF16,
                row_split=EPI_ROW_SPLIT, name="mla_uq")
    k = _matmul(c_kv, w_k[None], epi=_mla_k_epi, extras=(kg,), row_extras=(k_rope,), out_dtype=BF16,
                out_tn=MM_TN * MLA_QK_PAD // MLA_NOPE, row_split=EPI_ROW_SPLIT, name="mla_uk")
    v = _matmul(c_kv, w_v[None], out_dtype=BF16, name="mla_uv")
    o = _mla_attention(q, k, v, B, S)
    return _out_proj(o, w_out_all[j], x, mlp_gain, name="mla_out")


def _mlp(x, h, w1, w2_bf16, i):
    a = _matmul(h, w1, i, epi=_relu2_epi, out_dtype=BF16, name="mlp_up")
    return _matmul_ksplit(a, w2_bf16, i, x, name="mlp_down")


def kernel(x, positions, mix_norm, mlp_norm, sb_w_in, sb_w_out, fox_w_in, fox_b_f, fox_q_gain,
           fox_k_gain, fox_w_out, mla_w_in, mla_q_norm, mla_kv_norm, mla_w_uq, mla_w_ukv,
           mla_q_gain, mla_k_gain, mla_w_out, mlp_w1, mlp_w2):
    depth = mix_norm.shape[0]
    w2_bf16 = mlp_w2.astype(BF16)
    for i in range(depth):
        kind, j = i % N_MIXERS, i // N_MIXERS
        gain = mix_norm[i][None, :]
        mlp_gain = mlp_norm[i][None, :]
        if kind == 0:
            x, h = _sb_layer(x, gain, mlp_gain, sb_w_in, sb_w_out, j)
        elif kind == 1:
            x, h = _fox_layer(x, gain, mlp_gain, fox_w_in, fox_b_f[j], fox_q_gain[j], fox_k_gain[j],
                              fox_w_out, j)
        else:
            x, h = _mla_layer(x, positions, gain, mlp_gain, mla_w_in[j], mla_q_norm[j], mla_kv_norm[j],
                              mla_w_uq[j], mla_w_ukv[j], mla_q_gain[j], mla_k_gain[j], mla_w_out, j)
        x = _mlp(x, h, mlp_w1, w2_bf16, i)
    return x
```

```python
import functools
import math

import jax
import jax.numpy as jnp
from jax import lax
from jax.experimental import pallas as pl
from jax.experimental.pallas import tpu as pltpu

F32 = jnp.float32
BF16 = jnp.bfloat16

D_MODEL = 2048
HEAD_DIM = 128
N_HEADS = D_MODEL // HEAD_DIM
ATTN_WIDTH = N_HEADS * HEAD_DIM
D_FF = 4 * D_MODEL
MLA_Q_RANK = 3 * D_MODEL // 8
MLA_KV_RANK = D_MODEL // 4
MLA_NOPE = 128
MLA_ROPE = 64
MLA_V = 128
MLA_QK = MLA_NOPE + MLA_ROPE
MLA_QK_PAD = 256
ROPE_THETA = 10000.0
EPS = 1e-6
N_MIXERS = 3
LOG2E = math.log2(math.e)

LANES = 128
ATTN_TQ = 256
ATTN_TK = 256
SCORE_LOOKAHEAD = 5
ONES_ROWS = 16
PREP_AHEAD = 4
SCAN_LOOKAHEAD = 2
MM_TM = 1024
MM_TN = 1024
MLP_TK = 2048
OUT_TM = 512
OUT_ROW_SPLIT = 2
EPI_ROW_SPLIT = 4
VMEM_LIMIT = 48 * 1024 * 1024

_NT = (((1,), (1,)), ((), ()))


def _params(n_grid):
    return pltpu.CompilerParams(dimension_semantics=("arbitrary",) * n_grid,
                                vmem_limit_bytes=VMEM_LIMIT)


def _rms(x):
    ms = jnp.mean(x * x, axis=-1, keepdims=True)
    return x * lax.rsqrt(ms + EPS)


def _rope_pair(pair, cos_sin):
    prod = pair * cos_sin
    return prod + pltpu.roll(prod, MLA_ROPE, axis=1)


def _rot_columns(w_rope):
    half = w_rope.shape[-1] // 2
    return jnp.concatenate([-w_rope[..., half:], w_rope[..., :half]], axis=-1)


def _mm_kernel(*refs, norm, w_rows, epi, epi_tiles, n_extra, n_row_extra, has_res, emit_norm, row_split):
    it = iter(refs)
    x_ref = next(it)
    g_ref = next(it) if norm else None
    w_ref = next(it)
    extra = [next(it) for _ in range(n_extra)]
    row_extra = [next(it) for _ in range(n_row_extra)]
    res_ref = next(it) if has_res else None
    gn_ref = next(it) if emit_norm else None
    o_ref = next(it)
    hn_ref = next(it) if emit_norm else None
    h_ref = next(it) if norm else None

    if norm:
        @pl.when(pl.program_id(2) == 0)
        def _():
            h_ref[...] = (_rms(x_ref[0]) * g_ref[...]).astype(BF16)

    w = w_ref[0].astype(BF16)
    sub = o_ref.shape[1] // row_split
    active = None if epi_tiles is None else pl.program_id(2) < epi_tiles

    def product(r):
        rows = slice(r * sub, (r + 1) * sub)
        h = h_ref[rows, :] if norm else x_ref[0, rows, :]
        if w_rows:
            return lax.dot_general(h, w, _NT, preferred_element_type=F32)
        return jnp.dot(h, w, preferred_element_type=F32)

    pending = [product(0)]
    for r in range(row_split):
        if r + 1 < row_split:
            pending.append(product(r + 1))
        rows = slice(r * sub, (r + 1) * sub)
        acc = pending.pop(0)
        if has_res:
            acc = res_ref[0, rows, :] + acc
        if epi is not None:
            acc = epi(acc, *extra, *[e[0, rows, :] for e in row_extra], active=active)
        o_ref[0, rows, :] = acc.astype(o_ref.dtype)
        if emit_norm:
            hn_ref[0, rows, :] = (_rms(acc) * gn_ref[...]).astype(BF16)


def _matmul(x, w, layer=0, *, w_rows=False, n_cols=None, gain=None, epi=None, epi_tiles=None, extras=(),
            row_extras=(), res=None, next_gain=None, out_dtype, tm=MM_TM, tn=MM_TN, out_tn=None,
            row_split=1, name):
    B, S, K = x.shape
    N = w.shape[1 if w_rows else 2] if n_cols is None else n_cols
    out_tn = tn if out_tn is None else out_tn
    n_out = N // tn * out_tn
    norm = gain is not None
    emit_norm = next_gain is not None
    assert not emit_norm or tn == N
    grid = (B, S // tm, N // tn)
    tile = lambda b, i, j: (b, i, j)
    in_specs = [pl.BlockSpec((1, tm, K), lambda b, i, j: (b, i, 0))]
    args = [x]
    if norm:
        in_specs.append(pl.BlockSpec((1, K), lambda b, i, j: (0, 0)))
        args.append(gain)
    if w_rows:
        in_specs.append(pl.BlockSpec((1, tn, K), lambda b, i, j: (layer, j, 0)))
    else:
        in_specs.append(pl.BlockSpec((1, K, tn), lambda b, i, j: (layer, 0, j)))
    args.append(w)
    for e in extras:
        in_specs.append(pl.BlockSpec((1, out_tn), lambda b, i, j: (0, j)))
        args.append(e)
    for e in row_extras:
        in_specs.append(pl.BlockSpec((1, tm, e.shape[2]), lambda b, i, j: (b, i, 0)))
        args.append(e)
    if res is not None:
        in_specs.append(pl.BlockSpec((1, tm, tn), tile))
        args.append(res)
    out_specs = pl.BlockSpec((1, tm, out_tn), tile)
    out_shape = jax.ShapeDtypeStruct((B, S, n_out), out_dtype)
    if emit_norm:
        in_specs.append(pl.BlockSpec((1, N), lambda b, i, j: (0, 0)))
        args.append(next_gain)
        out_specs = [out_specs, pl.BlockSpec((1, tm, tn), tile)]
        out_shape = [out_shape, jax.ShapeDtypeStruct((B, S, N), BF16)]
    body = functools.partial(_mm_kernel, norm=norm, w_rows=w_rows, epi=epi, epi_tiles=epi_tiles,
                             n_extra=len(extras), n_row_extra=len(row_extras),
                             has_res=res is not None, emit_norm=emit_norm, row_split=row_split)
    return pl.pallas_call(
        body, grid=grid, in_specs=in_specs, out_specs=out_specs, out_shape=out_shape,
        scratch_shapes=[pltpu.VMEM((tm, K), BF16)] if norm else [],
        compiler_params=_params(3), name=name)(*args)


def _mm_ksplit_kernel(x_ref, w_ref, res_ref, o_ref):
    @pl.when(pl.program_id(3) == 0)
    def _():
        o_ref[0] = res_ref[0]

    o_ref[0] += jnp.dot(x_ref[0], w_ref[0].astype(BF16), preferred_element_type=F32)


def _matmul_ksplit(x, w, layer, res, *, tm=MM_TM, tn=MM_TN, tk=MLP_TK, name):
    B, S, K = x.shape
    N = w.shape[2]
    return pl.pallas_call(
        _mm_ksplit_kernel, grid=(B, S // tm, N // tn, K // tk),
        in_specs=[pl.BlockSpec((1, tm, tk), lambda b, i, j, k: (b, i, k)),
                  pl.BlockSpec((1, tk, tn), lambda b, i, j, k: (layer, k, j)),
                  pl.BlockSpec((1, tm, tn), lambda b, i, j, k: (b, i, j))],
        out_specs=pl.BlockSpec((1, tm, tn), lambda b, i, j, k: (b, i, j)),
        out_shape=jax.ShapeDtypeStruct((B, S, N), F32),
        compiler_params=_params(4), name=name)(x, w, res)


def _relu2_epi(acc, active=None):
    a = jnp.maximum(acc, 0.0)
    return a * a


def _head_norm_epi(acc, g_ref, active=None):
    g = g_ref[...]
    outs = []
    for c in range(acc.shape[1] // HEAD_DIM):
        sl = slice(c * HEAD_DIM, (c + 1) * HEAD_DIM)
        blk = acc[:, sl]
        inv = lax.rsqrt(jnp.mean(blk * blk, axis=-1, keepdims=True) + EPS)
        if active is not None:
            inv = jnp.where(active, inv, 1.0)
        outs.append(blk * inv * g[:, sl])
    return jnp.concatenate(outs, axis=1)


def _mla_head_norm(nope, roped_twice, g):
    sq = nope * nope + 0.5 * (roped_twice * roped_twice)
    inv = lax.rsqrt(jnp.sum(sq, axis=-1, keepdims=True) * (1.0 / MLA_QK) + EPS)
    return jnp.concatenate([nope, roped_twice], axis=1) * inv * g


def _mla_q_epi(acc, g_ref, cos_sin, active=None):
    g = g_ref[...]
    outs = []
    for c in range(acc.shape[1] // MLA_QK_PAD):
        head = acc[:, c * MLA_QK_PAD:(c + 1) * MLA_QK_PAD]
        roped = _rope_pair(head[:, MLA_NOPE:], cos_sin)
        outs.append(_mla_head_norm(head[:, :MLA_NOPE], roped, g[:, c * MLA_QK_PAD:(c + 1) * MLA_QK_PAD]))
    return jnp.concatenate(outs, axis=1)


def _mla_k_epi(acc, g_ref, k_rope, active=None):
    g = g_ref[...]
    outs = []
    for c in range(acc.shape[1] // MLA_NOPE):
        outs.append(_mla_head_norm(acc[:, c * MLA_NOPE:(c + 1) * MLA_NOPE], k_rope,
                                   g[:, c * MLA_QK_PAD:(c + 1) * MLA_QK_PAD]))
    return jnp.concatenate(outs, axis=1)


def _causal_tiles(S):
    return [(qi, list(range(qi + 1))) for qi in range(S // ATTN_TQ)]


def _tile_masks():
    key = lax.broadcasted_iota(jnp.int32, (ATTN_TK, ATTN_TQ), 0)
    qry = lax.broadcasted_iota(jnp.int32, (ATTN_TK, ATTN_TQ), 1)
    return key, qry


def _raw_scores(k_ref, q_ref, qi, ki):
    return lax.dot_general(k_ref[ki * ATTN_TK:(ki + 1) * ATTN_TK, :],
                           q_ref[qi * ATTN_TQ:(qi + 1) * ATTN_TQ, :], _NT,
                           preferred_element_type=F32)


def _transpose_v_tile(vT_ref, v_tile, ki):
    vt = v_tile(ki).T.astype(BF16)
    vT_ref[0:vt.shape[0], ki * ATTN_TK:(ki + 1) * ATTN_TK] = vt


class _score_issuer:
    def __init__(self, k_ref, q_ref, prep, n_blocks):
        self.k_ref, self.q_ref, self.prep, self.n_blocks = k_ref, q_ref, prep, n_blocks
        self.ready = 0
        for _ in range(min(PREP_AHEAD, n_blocks)):
            self._prepare_next()

    def _prepare_next(self):
        self.prep(self.ready)
        self.ready += 1

    def prepare_ahead(self, qi):
        if qi + PREP_AHEAD < self.n_blocks:
            self._prepare_next()

    def __call__(self, tile):
        qi, ki = tile
        assert max(qi, ki) < self.ready, "score tile issued before its rows were prepared"
        return _raw_scores(self.k_ref, self.q_ref, qi, ki)


def _softmax_sweep(q_ref, k_ref, vT_ref, o_ref, *, prep, scale, bias, S):
    tq, tk = ATTN_TQ, ATTN_TK
    c = scale * LOG2E
    key, qry = _tile_masks()
    causal = key <= qry
    blocks = _causal_tiles(S)
    order = [(qi, ki) for qi, kis in blocks for ki in kis]
    dv = vT_ref.shape[0] - ONES_ROWS
    vT_ref[dv:, :] = jnp.ones((ONES_ROWS, S), BF16)
    issue = _score_issuer(k_ref, q_ref, prep, len(blocks))
    pending = [issue(t) for t in order[:SCORE_LOOKAHEAD]]
    issued = SCORE_LOOKAHEAD
    for qi, kis in blocks:
        issue.prepare_ahead(qi)
        m = acc = None
        for ki in kis:
            s_raw = pending.pop(0)
            if issued < len(order):
                pending.append(issue(order[issued]))
                issued += 1
            s = s_raw * c
            b = bias(ki, qi)
            if b is not None:
                s = s + b
            if ki == qi:
                s = jnp.where(causal, s, -jnp.inf)
            s_max = jnp.max(s, axis=0, keepdims=True)
            m_new = s_max if m is None else jnp.maximum(m, s_max)
            p = jnp.exp2(s - m_new)
            pv = jnp.dot(vT_ref[:, ki * tk:(ki + 1) * tk], p.astype(BF16),
                         preferred_element_type=F32)
            acc = pv if m is None else jnp.exp2(m - m_new) * acc + pv
            m = m_new
        o = acc[:dv] / acc[dv:dv + 1]
        o_ref[0, qi * tq:(qi + 1) * tq, :] = o.T.astype(o_ref.dtype)


def _sb_attn_kernel(q_ref, k_ref, v_ref, u_ref, o_ref, vT_ref, *, scale, S):
    tq, tk = ATTN_TQ, ATTN_TK
    c = scale * LOG2E
    key, qry = _tile_masks()
    strict = key < qry
    blocks = _causal_tiles(S)
    order = [(qi, ki) for qi, kis in blocks for ki in reversed(kis)]
    q2, k2 = q_ref.at[0], k_ref.at[0]

    def v_tile(ki):
        return v_ref[0, ki * tk:(ki + 1) * tk, :].astype(F32)

    def gate(t, z_raw):
        qi, ki = t
        z = z_raw * c
        soft = jnp.log2(1.0 + jnp.exp2(-jnp.abs(z)))
        nlk = jnp.maximum(z, 0.0) + soft
        if ki == qi:
            nlk = jnp.where(strict, nlk, 0.0)
        hi = nlk.astype(BF16)
        lo = (nlk - hi.astype(F32)).astype(BF16)
        scan = jnp.dot(u_ref[...], jnp.concatenate([hi, lo], axis=0),
                       preferred_element_type=F32)
        return z, scan

    issue = _score_issuer(k2, q2, functools.partial(_transpose_v_tile, vT_ref, v_tile), len(blocks))
    scored = [issue(t) for t in order[:SCORE_LOOKAHEAD]]
    gated = []
    n_scored = [SCORE_LOOKAHEAD]
    n_gated = [0]

    def advance():
        if n_scored[0] < len(order):
            scored.append(issue(order[n_scored[0]]))
            n_scored[0] += 1
        if n_gated[0] < len(order):
            gated.append(gate(order[n_gated[0]], scored.pop(0)))
            n_gated[0] += 1

    for _ in range(SCAN_LOOKAHEAD):
        advance()
    for qi, kis in blocks:
        issue.prepare_ahead(qi)
        suf = acc = None
        for ki in reversed(kis):
            z, scan = gated.pop(0)
            advance()
            x = z + scan
            if suf is not None:
                x = x + suf
            a = jnp.exp2(x)
            if ki == qi:
                a = jnp.where(strict, a, 0.0)
            pv = jnp.dot(vT_ref[:, ki * tk:(ki + 1) * tk], a.astype(BF16),
                         preferred_element_type=F32)
            acc = pv if acc is None else acc + pv
            col = scan[0:1, :]
            suf = col if suf is None else suf + col
        o_ref[0, qi * tq:(qi + 1) * tq, :] = acc.T.astype(o_ref.dtype)


def _attention_call(body, args, in_specs, out_spec, out_shape, scratch_shapes, side_casts, B, name):
    n_in, n_side = len(args), len(side_casts)
    steps = B * N_HEADS
    in_specs, out_specs, out_shapes = list(in_specs), [out_spec], [out_shape]
    for w, layer in side_casts:
        _, R, C = w.shape
        rows = R // steps
        in_specs.append(pl.BlockSpec((1, rows, C), lambda b, h, layer=layer: (layer, b * N_HEADS + h, 0)))
        out_specs.append(pl.BlockSpec((rows, C), lambda b, h: (b * N_HEADS + h, 0)))
        out_shapes.append(jax.ShapeDtypeStruct((R, C), BF16))

    def kernel_body(*refs):
        ins, side_in = refs[:n_in], refs[n_in:n_in + n_side]
        o_ref, side_out = refs[n_in + n_side], refs[n_in + n_side + 1:n_in + 2 * n_side + 1]
        for src, dst in zip(side_in, side_out):
            dst[...] = src[0].astype(BF16)
        body(*ins, o_ref, *refs[n_in + 2 * n_side + 1:])

    return pl.pallas_call(
        kernel_body, grid=(B, N_HEADS), in_specs=in_specs, out_specs=out_specs, out_shape=out_shapes,
        scratch_shapes=scratch_shapes, compiler_params=_params(2), name=name)(
            *args, *[w for w, _ in side_casts])


def _sb_attention(qkv, B, S, side_casts):
    tk = ATTN_TK
    upper = (jnp.arange(tk)[None, :] >= jnp.arange(tk)[:, None])
    neg_u2 = jnp.tile(jnp.where(upper, -1.0, 0.0), (1, 2)).astype(BF16)
    H = N_HEADS
    return _attention_call(
        functools.partial(_sb_attn_kernel, scale=1.0 / math.sqrt(HEAD_DIM), S=S),
        (qkv, qkv, qkv, neg_u2),
        [pl.BlockSpec((1, S, HEAD_DIM), lambda b, h: (b, 0, h)),
         pl.BlockSpec((1, S, HEAD_DIM), lambda b, h: (b, 0, H + h)),
         pl.BlockSpec((1, S, HEAD_DIM), lambda b, h: (b, 0, 2 * H + h)),
         pl.BlockSpec((tk, 2 * tk), lambda b, h: (0, 0))],
        pl.BlockSpec((1, S, HEAD_DIM), lambda b, h: (b, 0, h)),
        jax.ShapeDtypeStruct((B, S, ATTN_WIDTH), BF16),
        [pltpu.VMEM((HEAD_DIM, S), BF16)], side_casts, B, "sb_attention")


def _fox_attn_kernel(q_ref, k_ref, v_ref, cfc_ref, cfr_ref, o_ref, vT_ref, slab_ref, *, scale, S):
    h = pl.program_id(1)

    def v_tile(ki):
        return v_ref[0, ki * ATTN_TK:(ki + 1) * ATTN_TK, :].astype(F32)

    def prep(c):
        rows = slice(c * ATTN_TK, (c + 1) * ATTN_TK)
        blk = cfc_ref[0, rows, :]
        lane = lax.broadcasted_iota(jnp.int32, blk.shape, 1)
        col = jnp.sum(jnp.where(lane == h, blk, 0.0), axis=1, keepdims=True) * LOG2E
        slab_ref[rows, :] = jnp.broadcast_to(col, blk.shape)
        _transpose_v_tile(vT_ref, v_tile, c)

    def bias(ki, qi):
        cf_q = cfr_ref[0, 0, :, qi * ATTN_TQ:(qi + 1) * ATTN_TQ] * LOG2E
        cf_k = slab_ref[ki * ATTN_TK:(ki + 1) * ATTN_TK, :]
        return cf_q - jnp.concatenate([cf_k] * (ATTN_TQ // LANES), axis=1)

    _softmax_sweep(q_ref.at[0], k_ref.at[0], vT_ref, o_ref, prep=prep, scale=scale, bias=bias, S=S)


def _fox_attention(qkv, cf_col, cf_row, B, S, side_casts):
    H = N_HEADS
    return _attention_call(
        functools.partial(_fox_attn_kernel, scale=1.0 / math.sqrt(HEAD_DIM), S=S),
        (qkv, qkv, qkv, cf_col, cf_row),
        [pl.BlockSpec((1, S, HEAD_DIM), lambda b, h: (b, 0, h)),
         pl.BlockSpec((1, S, HEAD_DIM), lambda b, h: (b, 0, H + h)),
         pl.BlockSpec((1, S, HEAD_DIM), lambda b, h: (b, 0, 2 * H + h)),
         pl.BlockSpec((1, S, LANES), lambda b, h: (b, 0, 0)),
         pl.BlockSpec((1, 1, 1, S), lambda b, h: (b, h, 0, 0))],
        pl.BlockSpec((1, S, HEAD_DIM), lambda b, h: (b, 0, h)),
        jax.ShapeDtypeStruct((B, S, ATTN_WIDTH), BF16),
        [pltpu.VMEM((HEAD_DIM + ONES_ROWS, S), BF16), pltpu.VMEM((S, LANES), F32)],
        side_casts, B, "fox_attention")


def _mla_attn_kernel(q_ref, k_ref, v_ref, o_ref, vT_scr, *, scale, S):
    def v_tile(ki):
        return v_ref[0, ki * ATTN_TK:(ki + 1) * ATTN_TK, :].astype(F32)

    _softmax_sweep(q_ref.at[0], k_ref.at[0], vT_scr, o_ref,
                   prep=functools.partial(_transpose_v_tile, vT_scr, v_tile),
                   scale=scale, bias=lambda ki, qi: None, S=S)


def _mla_attention(q, k, v, B, S, side_casts):
    return _attention_call(
        functools.partial(_mla_attn_kernel, scale=1.0 / math.sqrt(MLA_QK), S=S),
        (q, k, v),
        [pl.BlockSpec((1, S, MLA_QK_PAD), lambda b, h: (b, 0, h)),
         pl.BlockSpec((1, S, MLA_QK_PAD), lambda b, h: (b, 0, h)),
         pl.BlockSpec((1, S, MLA_V), lambda b, h: (b, 0, h))],
        pl.BlockSpec((1, S, MLA_V), lambda b, h: (b, 0, h)),
        jax.ShapeDtypeStruct((B, S, N_HEADS * MLA_V), BF16),
        [pltpu.VMEM((MLA_V + ONES_ROWS, S), BF16)], side_casts, B, "mla_attention")


def _fox_gate_kernel(x_ref, g_ref, w_ref, b_ref, tri_ref, col_ref, row_ref, carry_ref):
    @pl.when(pl.program_id(1) == 0)
    def _():
        carry_ref[...] = jnp.zeros(carry_ref.shape, F32)

    h = (_rms(x_ref[0]) * g_ref[...]).astype(BF16)
    f = jnp.dot(h, w_ref[...], preferred_element_type=F32) + b_ref[...]
    log_f = jnp.minimum(f, 0.0) - jnp.log1p(jnp.exp(-jnp.abs(f)))
    p0 = log_f.astype(BF16)
    r1 = log_f - p0.astype(F32)
    p1 = r1.astype(BF16)
    p2 = (r1 - p1.astype(F32)).astype(BF16)
    tri = tri_ref[...]
    cs = (jnp.dot(tri, p0, preferred_element_type=F32) + jnp.dot(tri, p1, preferred_element_type=F32)
          + jnp.dot(tri, p2, preferred_element_type=F32)) + carry_ref[...]
    col_ref[0] = cs
    row_ref[0] = cs.T
    carry_ref[...] = cs[cs.shape[0] - 1:, :]


def _fox_gate(x, gain, w_f, b_f, ts=512):
    B, S, K = x.shape
    tri = (jnp.arange(ts)[None, :] <= jnp.arange(ts)[:, None]).astype(BF16)
    return pl.pallas_call(
        _fox_gate_kernel, grid=(B, S // ts),
        in_specs=[pl.BlockSpec((1, ts, K), lambda b, i: (b, i, 0)),
                  pl.BlockSpec((1, K), lambda b, i: (0, 0)),
                  pl.BlockSpec((K, LANES), lambda b, i: (0, 0)),
                  pl.BlockSpec((1, LANES), lambda b, i: (0, 0)),
                  pl.BlockSpec((ts, ts), lambda b, i: (0, 0))],
        out_specs=[pl.BlockSpec((1, ts, LANES), lambda b, i: (b, i, 0)),
                   pl.BlockSpec((1, LANES, ts), lambda b, i: (b, 0, i))],
        out_shape=[jax.ShapeDtypeStruct((B, S, LANES), F32), jax.ShapeDtypeStruct((B, LANES, S), F32)],
        scratch_shapes=[pltpu.VMEM((1, LANES), F32)],
        compiler_params=_params(2), name="fox_gate")(x, gain, w_f, b_f, tri)


def _rope_table_kernel(pos_ref, inv_ref, cos_sin_ref):
    ang = pos_ref[0].astype(F32) * inv_ref[...]
    cos_sin_ref[0] = jnp.concatenate([jnp.cos(ang), jnp.sin(ang)], axis=1)


def _rope_table(positions):
    B, S = positions.shape
    half = MLA_ROPE // 2
    inv_freq = ROPE_THETA ** (-jnp.arange(0, half, dtype=F32) * 2.0 / MLA_ROPE)
    inv2 = jnp.concatenate([inv_freq, inv_freq])[None, :]
    return pl.pallas_call(
        _rope_table_kernel, grid=(B,),
        in_specs=[pl.BlockSpec((1, S, 1), lambda b: (b, 0, 0)),
                  pl.BlockSpec((1, MLA_ROPE), lambda b: (0, 0))],
        out_specs=pl.BlockSpec((1, S, 2 * MLA_ROPE), lambda b: (b, 0, 0)),
        out_shape=jax.ShapeDtypeStruct((B, S, 2 * MLA_ROPE), F32),
        compiler_params=_params(1), name="rope_table")(positions.reshape(B, S, 1), inv2)


def _mla_down_kernel(x_ref, g_ref, w_ref, qn_ref, kvn_ref, cos_sin_ref, cq_ref, ckv_ref, kr_ref):
    h = (_rms(x_ref[0]) * g_ref[...]).astype(BF16)
    down = jnp.dot(h, w_ref[...], preferred_element_type=F32)
    kv0 = MLA_Q_RANK
    r0 = MLA_Q_RANK + MLA_KV_RANK
    cq_ref[0] = (_rms(down[:, :kv0]) * qn_ref[...]).astype(BF16)
    ckv_ref[0] = (_rms(down[:, kv0:r0]) * kvn_ref[...]).astype(BF16)
    kr_ref[0] = _rope_pair(down[:, r0:r0 + 2 * MLA_ROPE], cos_sin_ref[0])


def _mla_down(x, gain, w_down, q_norm, kv_norm, cos_sin, tm=512):
    B, S, K = x.shape
    n_down = w_down.shape[1]
    row = lambda b, i: (b, i, 0)
    fixed = lambda b, i: (0, 0)
    return pl.pallas_call(
        _mla_down_kernel, grid=(B, S // tm),
        in_specs=[pl.BlockSpec((1, tm, K), row), pl.BlockSpec((1, K), fixed),
                  pl.BlockSpec((K, n_down), fixed),
                  pl.BlockSpec((1, MLA_Q_RANK), fixed), pl.BlockSpec((1, MLA_KV_RANK), fixed),
                  pl.BlockSpec((1, tm, 2 * MLA_ROPE), row)],
        out_specs=[pl.BlockSpec((1, tm, MLA_Q_RANK), row), pl.BlockSpec((1, tm, MLA_KV_RANK), row),
                   pl.BlockSpec((1, tm, 2 * MLA_ROPE), row)],
        out_shape=[jax.ShapeDtypeStruct((B, S, MLA_Q_RANK), BF16),
                   jax.ShapeDtypeStruct((B, S, MLA_KV_RANK), BF16),
                   jax.ShapeDtypeStruct((B, S, 2 * MLA_ROPE), F32)],
        compiler_params=_params(2), name="mla_down")(x, gain, w_down, q_norm, kv_norm, cos_sin)


def _out_proj(o, w_out, x, mlp_gain, *, name):
    return _matmul(o, w_out[None], res=x, next_gain=mlp_gain, out_dtype=F32, tm=OUT_TM,
                   tn=w_out.shape[1], row_split=OUT_ROW_SPLIT, name=name)


def _sb_layer(x, gain, mlp_gain, w_in, w_out, j, w2_cast):
    B, S, _ = x.shape
    qkv = _matmul(x, w_in, j, gain=gain, out_dtype=BF16, name="sb_qkv")
    o, w_out_bf16, w2_bf16 = _sb_attention(qkv, B, S, [(w_out, j), w2_cast])
    return (*_out_proj(o, w_out_bf16, x, mlp_gain, name="sb_out"), w2_bf16)


def _fox_layer(x, gain, mlp_gain, w_in, b_f, q_gain, k_gain, w_out, j, w2_cast):
    B, S, _ = x.shape
    w_f = jnp.pad(w_in[j, :, 3 * ATTN_WIDTH:], ((0, 0), (0, LANES - N_HEADS))).astype(BF16)
    b_pad = jnp.pad(b_f, (0, LANES - N_HEADS))[None, :]
    qkv_gain = jnp.concatenate([jnp.tile(q_gain, N_HEADS), jnp.tile(k_gain, N_HEADS),
                                jnp.ones((ATTN_WIDTH,), F32)])[None, :]
    qkv = _matmul(x, jnp.swapaxes(w_in, 1, 2), j, w_rows=True, n_cols=3 * ATTN_WIDTH, gain=gain,
                  epi=_head_norm_epi, epi_tiles=2 * ATTN_WIDTH // MM_TN, extras=(qkv_gain,),
                  out_dtype=BF16, row_split=OUT_ROW_SPLIT, name="fox_qkv")
    cf_col, cf_rowT = _fox_gate(x, gain, w_f, b_pad)
    cf_row = cf_rowT[:, :N_HEADS, :].reshape(B, N_HEADS, 1, S)
    o, w_out_bf16, w2_bf16 = _fox_attention(qkv, cf_col, cf_row, B, S, [(w_out, j), w2_cast])
    return (*_out_proj(o, w_out_bf16, x, mlp_gain, name="fox_out"), w2_bf16)


def _mla_layer(x, positions, gain, mlp_gain, w_in, q_norm, kv_norm, w_uq, w_ukv, q_gain, k_gain,
               w_out_all, j, w2_cast):
    B, S, _ = x.shape
    r0 = MLA_Q_RANK + MLA_KV_RANK
    w_down = jnp.concatenate([w_in, _rot_columns(w_in[:, r0:])], axis=1).astype(BF16)
    w_q = w_uq.reshape(MLA_Q_RANK, N_HEADS, MLA_QK)
    w_q = jnp.concatenate([w_q, _rot_columns(w_q[:, :, MLA_NOPE:])], axis=2)
    w_q = w_q.reshape(MLA_Q_RANK, -1).astype(BF16)
    w_kv = w_ukv.reshape(MLA_KV_RANK, N_HEADS, MLA_NOPE + MLA_V)
    w_k = w_kv[:, :, :MLA_NOPE].reshape(MLA_KV_RANK, -1).astype(BF16)
    w_v = w_kv[:, :, MLA_NOPE:].reshape(MLA_KV_RANK, -1).astype(BF16)
    qg = jnp.tile(jnp.pad(q_gain, (0, MLA_QK_PAD - MLA_QK)), N_HEADS)[None, :]
    kg = jnp.tile(jnp.pad(k_gain, (0, MLA_QK_PAD - MLA_QK)), N_HEADS)[None, :]

    cos_sin = _rope_table(positions)
    c_q, c_kv, k_rope = _mla_down(x, gain, w_down, q_norm[None, :], kv_norm[None, :], cos_sin)
    q = _matmul(c_q, w_q[None], epi=_mla_q_epi, extras=(qg,), row_extras=(cos_sin,), out_dtype=BF16,
                row_split=EPI_ROW_SPLIT, name="mla_uq")
    k = _matmul(c_kv, w_k[None], epi=_mla_k_epi, extras=(kg,), row_extras=(k_rope,), out_dtype=BF16,
                out_tn=MM_TN * MLA_QK_PAD // MLA_NOPE, row_split=EPI_ROW_SPLIT, name="mla_uk")
    v = _matmul(c_kv, w_v[None], out_dtype=BF16, name="mla_uv")
    o, w_out_bf16, w2_bf16 = _mla_attention(q, k, v, B, S, [(w_out_all, j), w2_cast])
    return (*_out_proj(o, w_out_bf16, x, mlp_gain, name="mla_out"), w2_bf16)


def _mlp(x, h, w1, w2_bf16, i):
    a = _matmul(h, w1, i, epi=_relu2_epi, out_dtype=BF16, name="mlp_up")
    return _matmul_ksplit(a, w2_bf16[None], 0, x, name="mlp_down")


def kernel(x, positions, mix_norm, mlp_norm, sb_w_in, sb_w_out, fox_w_in, fox_b_f, fox_q_gain,
           fox_k_gain, fox_w_out, mla_w_in, mla_q_norm, mla_kv_norm, mla_w_uq, mla_w_ukv,
           mla_q_gain, mla_k_gain, mla_w_out, mlp_w1, mlp_w2):
    depth = mix_norm.shape[0]
    for i in range(depth):
        kind, j = i % N_MIXERS, i // N_MIXERS
        gain = mix_norm[i][None, :]
        mlp_gain = mlp_norm[i][None, :]
        w2_cast = (mlp_w2, i)
        if kind == 0:
            x, h, w2_bf16 = _sb_layer(x, gain, mlp_gain, sb_w_in, sb_w_out, j, w2_cast)
        elif kind == 1:
            x, h, w2_bf16 = _fox_layer(x, gain, mlp_gain, fox_w_in, fox_b_f[j], fox_q_gain[j],
                                       fox_k_gain[j], fox_w_out, j, w2_cast)
        else:
            x, h, w2_bf16 = _mla_layer(x, positions, gain, mlp_gain, mla_w_in[j], mla_q_norm[j],
                                       mla_kv_norm[j], mla_w_uq[j], mla_w_ukv[j], mla_q_gain[j],
                                       mla_k_gain[j], mla_w_out, j, w2_cast)
        x = _mlp(x, h, mlp_w1, w2_bf16, i)
    return x
```

```python
import functools
import math

import jax
import jax.numpy as jnp
from jax import lax
from jax.experimental import pallas as pl
from jax.experimental.pallas import tpu as pltpu

F32 = jnp.float32
BF16 = jnp.bfloat16

D_MODEL = 2048
HEAD_DIM = 128
N_HEADS = D_MODEL // HEAD_DIM
ATTN_WIDTH = N_HEADS * HEAD_DIM
D_FF = 4 * D_MODEL
MLA_Q_RANK = 3 * D_MODEL // 8
MLA_KV_RANK = D_MODEL // 4
MLA_NOPE = 128
MLA_ROPE = 64
MLA_V = 128
MLA_QK = MLA_NOPE + MLA_ROPE
MLA_QK_PAD = 256
ROPE_THETA = 10000.0
EPS = 1e-6
N_MIXERS = 3
LOG2E = math.log2(math.e)

LANES = 128
ATTN_TQ = 256
ATTN_TK = 256
SCORE_LOOKAHEAD = 5
ONES_ROWS = 16
PREP_AHEAD = 4
SCAN_LOOKAHEAD = 2
MM_TM = 1024
MM_TN = 1024
MLP_UP_TM = 2048
MLP_TK = 2048
OUT_TM = 512
OUT_ROW_SPLIT = 2
EPI_ROW_SPLIT = 4
VMEM_LIMIT = 56 * 1024 * 1024

_NT = (((1,), (1,)), ((), ()))


def _params(n_grid):
    return pltpu.CompilerParams(dimension_semantics=("arbitrary",) * n_grid,
                                vmem_limit_bytes=VMEM_LIMIT)


def _rms(x):
    ms = jnp.mean(x * x, axis=-1, keepdims=True)
    return x * lax.rsqrt(ms + EPS)


def _rope_pair(pair, cos_sin):
    prod = pair * cos_sin
    return prod + pltpu.roll(prod, MLA_ROPE, axis=1)


def _rot_columns(w_rope):
    half = w_rope.shape[-1] // 2
    return jnp.concatenate([-w_rope[..., half:], w_rope[..., :half]], axis=-1)


def _mm_kernel(*refs, norm, w_rows, epi, epi_tiles, n_extra, n_row_extra, has_res, emit_norm, row_split):
    it = iter(refs)
    x_ref = next(it)
    g_ref = next(it) if norm else None
    w_ref = next(it)
    extra = [next(it) for _ in range(n_extra)]
    row_extra = [next(it) for _ in range(n_row_extra)]
    res_ref = next(it) if has_res else None
    gn_ref = next(it) if emit_norm else None
    o_ref = next(it)
    hn_ref = next(it) if emit_norm else None
    h_ref = next(it) if norm else None

    if norm:
        @pl.when(pl.program_id(2) == 0)
        def _():
            h_ref[...] = (_rms(x_ref[0]) * g_ref[...]).astype(BF16)

    w = w_ref[0].astype(BF16)
    sub = o_ref.shape[1] // row_split
    active = None if epi_tiles is None else pl.program_id(2) < epi_tiles

    def product(r):
        rows = slice(r * sub, (r + 1) * sub)
        h = h_ref[rows, :] if norm else x_ref[0, rows, :]
        if w_rows:
            return lax.dot_general(h, w, _NT, preferred_element_type=F32)
        return jnp.dot(h, w, preferred_element_type=F32)

    pending = [product(0)]
    for r in range(row_split):
        if r + 1 < row_split:
            pending.append(product(r + 1))
        rows = slice(r * sub, (r + 1) * sub)
        acc = pending.pop(0)
        if has_res:
            acc = res_ref[0, rows, :] + acc
        if epi is not None:
            acc = epi(acc, *extra, *[e[0, rows, :] for e in row_extra], active=active)
        o_ref[0, rows, :] = acc.astype(o_ref.dtype)
        if emit_norm:
            hn_ref[0, rows, :] = (_rms(acc) * gn_ref[...]).astype(BF16)


def _matmul(x, w, layer=0, *, w_rows=False, n_cols=None, gain=None, epi=None, epi_tiles=None, extras=(),
            row_extras=(), res=None, next_gain=None, out_dtype, tm=MM_TM, tn=MM_TN, out_tn=None,
            row_split=1, name):
    B, S, K = x.shape
    N = w.shape[1 if w_rows else 2] if n_cols is None else n_cols
    out_tn = tn if out_tn is None else out_tn
    n_out = N // tn * out_tn
    norm = gain is not None
    emit_norm = next_gain is not None
    assert not emit_norm or tn == N
    grid = (B, S // tm, N // tn)
    tile = lambda b, i, j: (b, i, j)
    in_specs = [pl.BlockSpec((1, tm, K), lambda b, i, j: (b, i, 0))]
    args = [x]
    if norm:
        in_specs.append(pl.BlockSpec((1, K), lambda b, i, j: (0, 0)))
        args.append(gain)
    if w_rows:
        in_specs.append(pl.BlockSpec((1, tn, K), lambda b, i, j: (layer, j, 0)))
    else:
        in_specs.append(pl.BlockSpec((1, K, tn), lambda b, i, j: (layer, 0, j)))
    args.append(w)
    for e in extras:
        in_specs.append(pl.BlockSpec((1, out_tn), lambda b, i, j: (0, j)))
        args.append(e)
    for e in row_extras:
        in_specs.append(pl.BlockSpec((1, tm, e.shape[2]), lambda b, i, j: (b, i, 0)))
        args.append(e)
    if res is not None:
        in_specs.append(pl.BlockSpec((1, tm, tn), tile))
        args.append(res)
    out_specs = pl.BlockSpec((1, tm, out_tn), tile)
    out_shape = jax.ShapeDtypeStruct((B, S, n_out), out_dtype)
    if emit_norm:
        in_specs.append(pl.BlockSpec((1, N), lambda b, i, j: (0, 0)))
        args.append(next_gain)
        out_specs = [out_specs, pl.BlockSpec((1, tm, tn), tile)]
        out_shape = [out_shape, jax.ShapeDtypeStruct((B, S, N), BF16)]
    body = functools.partial(_mm_kernel, norm=norm, w_rows=w_rows, epi=epi, epi_tiles=epi_tiles,
                             n_extra=len(extras), n_row_extra=len(row_extras),
                             has_res=res is not None, emit_norm=emit_norm, row_split=row_split)
    return pl.pallas_call(
        body, grid=grid, in_specs=in_specs, out_specs=out_specs, out_shape=out_shape,
        scratch_shapes=[pltpu.VMEM((tm, K), BF16)] if norm else [],
        compiler_params=_params(3), name=name)(*args)


def _mm_ksplit_kernel(x_ref, w_ref, res_ref, o_ref):
    @pl.when(pl.program_id(3) == 0)
    def _():
        o_ref[0] = res_ref[0]

    o_ref[0] += jnp.dot(x_ref[0], w_ref[0].astype(BF16), preferred_element_type=F32)


def _matmul_ksplit(x, w, layer, res, *, tm=MM_TM, tn=MM_TN, tk=MLP_TK, name):
    B, S, K = x.shape
    N = w.shape[2]
    return pl.pallas_call(
        _mm_ksplit_kernel, grid=(B, S // tm, N // tn, K // tk),
        in_specs=[pl.BlockSpec((1, tm, tk), lambda b, i, j, k: (b, i, k)),
                  pl.BlockSpec((1, tk, tn), lambda b, i, j, k: (layer, k, j)),
                  pl.BlockSpec((1, tm, tn), lambda b, i, j, k: (b, i, j))],
        out_specs=pl.BlockSpec((1, tm, tn), lambda b, i, j, k: (b, i, j)),
        out_shape=jax.ShapeDtypeStruct((B, S, N), F32),
        compiler_params=_params(4), name=name)(x, w, res)


def _relu2_epi(acc, active=None):
    a = jnp.maximum(acc, 0.0)
    return a * a


def _head_norm_epi(acc, g_ref, active=None):
    g = g_ref[...]
    outs = []
    for c in range(acc.shape[1] // HEAD_DIM):
        sl = slice(c * HEAD_DIM, (c + 1) * HEAD_DIM)
        blk = acc[:, sl]
        inv = lax.rsqrt(jnp.mean(blk * blk, axis=-1, keepdims=True) + EPS)
        if active is not None:
            inv = jnp.where(active, inv, 1.0)
        outs.append(blk * inv * g[:, sl])
    return jnp.concatenate(outs, axis=1)


def _mla_head_norm(nope, roped_twice, g):
    sq = nope * nope + 0.5 * (roped_twice * roped_twice)
    inv = lax.rsqrt(jnp.sum(sq, axis=-1, keepdims=True) * (1.0 / MLA_QK) + EPS)
    return jnp.concatenate([nope, roped_twice], axis=1) * inv * g


def _mla_q_epi(acc, g_ref, cos_sin, active=None):
    g = g_ref[...]
    outs = []
    for c in range(acc.shape[1] // MLA_QK_PAD):
        head = acc[:, c * MLA_QK_PAD:(c + 1) * MLA_QK_PAD]
        roped = _rope_pair(head[:, MLA_NOPE:], cos_sin)
        outs.append(_mla_head_norm(head[:, :MLA_NOPE], roped, g[:, c * MLA_QK_PAD:(c + 1) * MLA_QK_PAD]))
    return jnp.concatenate(outs, axis=1)


def _mla_k_epi(acc, g_ref, k_rope, active=None):
    g = g_ref[...]
    outs = []
    for c in range(acc.shape[1] // MLA_NOPE):
        outs.append(_mla_head_norm(acc[:, c * MLA_NOPE:(c + 1) * MLA_NOPE], k_rope,
                                   g[:, c * MLA_QK_PAD:(c + 1) * MLA_QK_PAD]))
    return jnp.concatenate(outs, axis=1)


def _causal_tiles(S):
    return [(qi, list(range(qi + 1))) for qi in range(S // ATTN_TQ)]


def _tile_masks():
    key = lax.broadcasted_iota(jnp.int32, (ATTN_TK, ATTN_TQ), 0)
    qry = lax.broadcasted_iota(jnp.int32, (ATTN_TK, ATTN_TQ), 1)
    return key, qry


def _raw_scores(k_ref, q_ref, qi, ki):
    return lax.dot_general(k_ref[ki * ATTN_TK:(ki + 1) * ATTN_TK, :],
                           q_ref[qi * ATTN_TQ:(qi + 1) * ATTN_TQ, :], _NT,
                           preferred_element_type=F32)


def _transpose_v_tile(vT_ref, v_tile, ki):
    vt = v_tile(ki).T.astype(BF16)
    vT_ref[0:vt.shape[0], ki * ATTN_TK:(ki + 1) * ATTN_TK] = vt


class _score_issuer:
    def __init__(self, k_ref, q_ref, prep, n_blocks):
        self.k_ref, self.q_ref, self.prep, self.n_blocks = k_ref, q_ref, prep, n_blocks
        self.ready = 0
        for _ in range(min(PREP_AHEAD, n_blocks)):
            self._prepare_next()

    def _prepare_next(self):
        self.prep(self.ready)
        self.ready += 1

    def prepare_ahead(self, qi):
        if qi + PREP_AHEAD < self.n_blocks:
            self._prepare_next()

    def __call__(self, tile):
        qi, ki = tile
        assert max(qi, ki) < self.ready, "score tile issued before its rows were prepared"
        return _raw_scores(self.k_ref, self.q_ref, qi, ki)


def _softmax_sweep(q_ref, k_ref, vT_ref, o_ref, *, prep, scale, bias, S):
    tq, tk = ATTN_TQ, ATTN_TK
    c = scale * LOG2E
    key, qry = _tile_masks()
    causal = key <= qry
    blocks = _causal_tiles(S)
    order = [(qi, ki) for qi, kis in blocks for ki in kis]
    dv = vT_ref.shape[0] - ONES_ROWS
    vT_ref[dv:, :] = jnp.ones((ONES_ROWS, S), BF16)
    issue = _score_issuer(k_ref, q_ref, prep, len(blocks))
    pending = [issue(t) for t in order[:SCORE_LOOKAHEAD]]
    issued = SCORE_LOOKAHEAD
    for qi, kis in blocks:
        issue.prepare_ahead(qi)
        m = acc = None
        for ki in kis:
            s_raw = pending.pop(0)
            if issued < len(order):
                pending.append(issue(order[issued]))
                issued += 1
            s = s_raw * c
            b = bias(ki, qi)
            if b is not None:
                s = s + b
            if ki == qi:
                s = jnp.where(causal, s, -jnp.inf)
            s_max = jnp.max(s, axis=0, keepdims=True)
            m_new = s_max if m is None else jnp.maximum(m, s_max)
            p = jnp.exp2(s - m_new)
            pv = jnp.dot(vT_ref[:, ki * tk:(ki + 1) * tk], p.astype(BF16),
                         preferred_element_type=F32)
            acc = pv if m is None else jnp.exp2(m - m_new) * acc + pv
            m = m_new
        o = acc[:dv] / acc[dv:dv + 1]
        o_ref[0, qi * tq:(qi + 1) * tq, :] = o.T.astype(o_ref.dtype)


def _sb_attn_kernel(q_ref, k_ref, v_ref, u_ref, o_ref, vT_ref, *, scale, S):
    tq, tk = ATTN_TQ, ATTN_TK
    c = scale * LOG2E
    key, qry = _tile_masks()
    strict = key < qry
    blocks = _causal_tiles(S)
    order = [(qi, ki) for qi, kis in blocks for ki in reversed(kis)]
    q2, k2 = q_ref.at[0], k_ref.at[0]

    def v_tile(ki):
        return v_ref[0, ki * tk:(ki + 1) * tk, :].astype(F32)

    def gate(t, z_raw):
        qi, ki = t
        z = z_raw * c
        soft = jnp.log2(1.0 + jnp.exp2(-jnp.abs(z)))
        nlk = jnp.maximum(z, 0.0) + soft
        if ki == qi:
            nlk = jnp.where(strict, nlk, 0.0)
        hi = nlk.astype(BF16)
        lo = (nlk - hi.astype(F32)).astype(BF16)
        scan = jnp.dot(u_ref[...], jnp.concatenate([hi, lo], axis=0),
                       preferred_element_type=F32)
        return z, scan

    issue = _score_issuer(k2, q2, functools.partial(_transpose_v_tile, vT_ref, v_tile), len(blocks))
    scored = [issue(t) for t in order[:SCORE_LOOKAHEAD]]
    gated = []
    n_scored = [SCORE_LOOKAHEAD]
    n_gated = [0]

    def advance():
        if n_scored[0] < len(order):
            scored.append(issue(order[n_scored[0]]))
            n_scored[0] += 1
        if n_gated[0] < len(order):
            gated.append(gate(order[n_gated[0]], scored.pop(0)))
            n_gated[0] += 1

    for _ in range(SCAN_LOOKAHEAD):
        advance()
    for qi, kis in blocks:
        issue.prepare_ahead(qi)
        suf = acc = None
        for ki in reversed(kis):
            z, scan = gated.pop(0)
            advance()
            x = z + scan
            if suf is not None:
                x = x + suf
            a = jnp.exp2(x)
            if ki == qi:
                a = jnp.where(strict, a, 0.0)
            pv = jnp.dot(vT_ref[:, ki * tk:(ki + 1) * tk], a.astype(BF16),
                         preferred_element_type=F32)
            acc = pv if acc is None else acc + pv
            col = scan[0:1, :]
            suf = col if suf is None else suf + col
        o_ref[0, qi * tq:(qi + 1) * tq, :] = acc.T.astype(o_ref.dtype)


def _attention_call(body, args, in_specs, out_spec, out_shape, scratch_shapes, side_casts, B, name):
    n_in, n_side = len(args), len(side_casts)
    steps = B * N_HEADS
    in_specs, out_specs, out_shapes = list(in_specs), [out_spec], [out_shape]
    for w, layer in side_casts:
        _, R, C = w.shape
        rows = R // steps
        in_specs.append(pl.BlockSpec((1, rows, C), lambda b, h, layer=layer: (layer, b * N_HEADS + h, 0)))
        out_specs.append(pl.BlockSpec((rows, C), lambda b, h: (b * N_HEADS + h, 0)))
        out_shapes.append(jax.ShapeDtypeStruct((R, C), BF16))

    def kernel_body(*refs):
        ins, side_in = refs[:n_in], refs[n_in:n_in + n_side]
        o_ref, side_out = refs[n_in + n_side], refs[n_in + n_side + 1:n_in + 2 * n_side + 1]
        for src, dst in zip(side_in, side_out):
            dst[...] = src[0].astype(BF16)
        body(*ins, o_ref, *refs[n_in + 2 * n_side + 1:])

    return pl.pallas_call(
        kernel_body, grid=(B, N_HEADS), in_specs=in_specs, out_specs=out_specs, out_shape=out_shapes,
        scratch_shapes=scratch_shapes, compiler_params=_params(2), name=name)(
            *args, *[w for w, _ in side_casts])


def _sb_attention(qkv, B, S, side_casts):
    tk = ATTN_TK
    upper = (jnp.arange(tk)[None, :] >= jnp.arange(tk)[:, None])
    neg_u2 = jnp.tile(jnp.where(upper, -1.0, 0.0), (1, 2)).astype(BF16)
    H = N_HEADS
    return _attention_call(
        functools.partial(_sb_attn_kernel, scale=1.0 / math.sqrt(HEAD_DIM), S=S),
        (qkv, qkv, qkv, neg_u2),
        [pl.BlockSpec((1, S, HEAD_DIM), lambda b, h: (b, 0, h)),
         pl.BlockSpec((1, S, HEAD_DIM), lambda b, h: (b, 0, H + h)),
         pl.BlockSpec((1, S, HEAD_DIM), lambda b, h: (b, 0, 2 * H + h)),
         pl.BlockSpec((tk, 2 * tk), lambda b, h: (0, 0))],
        pl.BlockSpec((1, S, HEAD_DIM), lambda b, h: (b, 0, h)),
        jax.ShapeDtypeStruct((B, S, ATTN_WIDTH), BF16),
        [pltpu.VMEM((HEAD_DIM, S), BF16)], side_casts, B, "sb_attention")


def _fox_attn_kernel(q_ref, k_ref, v_ref, cfc_ref, cfr_ref, o_ref, vT_ref, slab_ref, *, scale, S):
    h = pl.program_id(1)

    def v_tile(ki):
        return v_ref[0, ki * ATTN_TK:(ki + 1) * ATTN_TK, :].astype(F32)

    def prep(c):
        rows = slice(c * ATTN_TK, (c + 1) * ATTN_TK)
        blk = cfc_ref[0, rows, :]
        lane = lax.broadcasted_iota(jnp.int32, blk.shape, 1)
        col = jnp.sum(jnp.where(lane == h, blk, 0.0), axis=1, keepdims=True) * LOG2E
        slab_ref[rows, :] = jnp.broadcast_to(col, blk.shape)
        _transpose_v_tile(vT_ref, v_tile, c)

    def bias(ki, qi):
        cf_q = cfr_ref[0, 0, :, qi * ATTN_TQ:(qi + 1) * ATTN_TQ] * LOG2E
        cf_k = slab_ref[ki * ATTN_TK:(ki + 1) * ATTN_TK, :]
        return cf_q - jnp.concatenate([cf_k] * (ATTN_TQ // LANES), axis=1)

    _softmax_sweep(q_ref.at[0], k_ref.at[0], vT_ref, o_ref, prep=prep, scale=scale, bias=bias, S=S)


def _fox_attention(qkv, cf_col, cf_row, B, S, side_casts):
    H = N_HEADS
    return _attention_call(
        functools.partial(_fox_attn_kernel, scale=1.0 / math.sqrt(HEAD_DIM), S=S),
        (qkv, qkv, qkv, cf_col, cf_row),
        [pl.BlockSpec((1, S, HEAD_DIM), lambda b, h: (b, 0, h)),
         pl.BlockSpec((1, S, HEAD_DIM), lambda b, h: (b, 0, H + h)),
         pl.BlockSpec((1, S, HEAD_DIM), lambda b, h: (b, 0, 2 * H + h)),
         pl.BlockSpec((1, S, LANES), lambda b, h: (b, 0, 0)),
         pl.BlockSpec((1, 1, 1, S), lambda b, h: (b, h, 0, 0))],
        pl.BlockSpec((1, S, HEAD_DIM), lambda b, h: (b, 0, h)),
        jax.ShapeDtypeStruct((B, S, ATTN_WIDTH), BF16),
        [pltpu.VMEM((HEAD_DIM + ONES_ROWS, S), BF16), pltpu.VMEM((S, LANES), F32)],
        side_casts, B, "fox_attention")


def _mla_attn_kernel(q_ref, k_ref, v_ref, o_ref, vT_scr, *, scale, S):
    def v_tile(ki):
        return v_ref[0, ki * ATTN_TK:(ki + 1) * ATTN_TK, :].astype(F32)

    _softmax_sweep(q_ref.at[0], k_ref.at[0], vT_scr, o_ref,
                   prep=functools.partial(_transpose_v_tile, vT_scr, v_tile),
                   scale=scale, bias=lambda ki, qi: None, S=S)


def _mla_attention(q, k, v, B, S, side_casts):
    return _attention_call(
        functools.partial(_mla_attn_kernel, scale=1.0 / math.sqrt(MLA_QK), S=S),
        (q, k, v),
        [pl.BlockSpec((1, S, MLA_QK_PAD), lambda b, h: (b, 0, h)),
         pl.BlockSpec((1, S, MLA_QK_PAD), lambda b, h: (b, 0, h)),
         pl.BlockSpec((1, S, MLA_V), lambda b, h: (b, 0, h))],
        pl.BlockSpec((1, S, MLA_V), lambda b, h: (b, 0, h)),
        jax.ShapeDtypeStruct((B, S, N_HEADS * MLA_V), BF16),
        [pltpu.VMEM((MLA_V + ONES_ROWS, S), BF16)], side_casts, B, "mla_attention")


def _fox_gate_kernel(x_ref, g_ref, w_ref, b_ref, tri_ref, col_ref, row_ref, carry_ref):
    @pl.when(pl.program_id(1) == 0)
    def _():
        carry_ref[...] = jnp.zeros(carry_ref.shape, F32)

    h = (_rms(x_ref[0]) * g_ref[...]).astype(BF16)
    f = jnp.dot(h, w_ref[...], preferred_element_type=F32) + b_ref[...]
    log_f = jnp.minimum(f, 0.0) - jnp.log1p(jnp.exp(-jnp.abs(f)))
    p0 = log_f.astype(BF16)
    r1 = log_f - p0.astype(F32)
    p1 = r1.astype(BF16)
    p2 = (r1 - p1.astype(F32)).astype(BF16)
    tri = tri_ref[...]
    cs = (jnp.dot(tri, p0, preferred_element_type=F32) + jnp.dot(tri, p1, preferred_element_type=F32)
          + jnp.dot(tri, p2, preferred_element_type=F32)) + carry_ref[...]
    col_ref[0] = cs
    row_ref[0] = cs.T
    carry_ref[...] = cs[cs.shape[0] - 1:, :]


def _fox_gate(x, gain, w_f, b_f, ts=512):
    B, S, K = x.shape
    tri = (jnp.arange(ts)[None, :] <= jnp.arange(ts)[:, None]).astype(BF16)
    return pl.pallas_call(
        _fox_gate_kernel, grid=(B, S // ts),
        in_specs=[pl.BlockSpec((1, ts, K), lambda b, i: (b, i, 0)),
                  pl.BlockSpec((1, K), lambda b, i: (0, 0)),
                  pl.BlockSpec((K, LANES), lambda b, i: (0, 0)),
                  pl.BlockSpec((1, LANES), lambda b, i: (0, 0)),
                  pl.BlockSpec((ts, ts), lambda b, i: (0, 0))],
        out_specs=[pl.BlockSpec((1, ts, LANES), lambda b, i: (b, i, 0)),
                   pl.BlockSpec((1, LANES, ts), lambda b, i: (b, 0, i))],
        out_shape=[jax.ShapeDtypeStruct((B, S, LANES), F32), jax.ShapeDtypeStruct((B, LANES, S), F32)],
        scratch_shapes=[pltpu.VMEM((1, LANES), F32)],
        compiler_params=_params(2), name="fox_gate")(x, gain, w_f, b_f, tri)


def _rope_table_kernel(pos_ref, inv_ref, cos_sin_ref):
    ang = pos_ref[0].astype(F32) * inv_ref[...]
    cos_sin_ref[0] = jnp.concatenate([jnp.cos(ang), jnp.sin(ang)], axis=1)


def _rope_table(positions):
    B, S = positions.shape
    half = MLA_ROPE // 2
    inv_freq = ROPE_THETA ** (-jnp.arange(0, half, dtype=F32) * 2.0 / MLA_ROPE)
    inv2 = jnp.concatenate([inv_freq, inv_freq])[None, :]
    return pl.pallas_call(
        _rope_table_kernel, grid=(B,),
        in_specs=[pl.BlockSpec((1, S, 1), lambda b: (b, 0, 0)),
                  pl.BlockSpec((1, MLA_ROPE), lambda b: (0, 0))],
        out_specs=pl.BlockSpec((1, S, 2 * MLA_ROPE), lambda b: (b, 0, 0)),
        out_shape=jax.ShapeDtypeStruct((B, S, 2 * MLA_ROPE), F32),
        compiler_params=_params(1), name="rope_table")(positions.reshape(B, S, 1), inv2)


def _mla_down_kernel(x_ref, g_ref, w_ref, qn_ref, kvn_ref, cos_sin_ref, cq_ref, ckv_ref, kr_ref):
    h = (_rms(x_ref[0]) * g_ref[...]).astype(BF16)
    down = lax.dot_general(h, w_ref[...], _NT, preferred_element_type=F32)
    kv0 = MLA_Q_RANK
    r0 = MLA_Q_RANK + MLA_KV_RANK
    cq_ref[0] = (_rms(down[:, :kv0]) * qn_ref[...]).astype(BF16)
    ckv_ref[0] = (_rms(down[:, kv0:r0]) * kvn_ref[...]).astype(BF16)
    kr_ref[0] = _rope_pair(down[:, r0:r0 + 2 * MLA_ROPE], cos_sin_ref[0])


def _mla_down(x, gain, w_down_t, q_norm, kv_norm, cos_sin, tm=512):
    B, S, K = x.shape
    n_down = w_down_t.shape[0]
    row = lambda b, i: (b, i, 0)
    fixed = lambda b, i: (0, 0)
    return pl.pallas_call(
        _mla_down_kernel, grid=(B, S // tm),
        in_specs=[pl.BlockSpec((1, tm, K), row), pl.BlockSpec((1, K), fixed),
                  pl.BlockSpec((n_down, K), fixed),
                  pl.BlockSpec((1, MLA_Q_RANK), fixed), pl.BlockSpec((1, MLA_KV_RANK), fixed),
                  pl.BlockSpec((1, tm, 2 * MLA_ROPE), row)],
        out_specs=[pl.BlockSpec((1, tm, MLA_Q_RANK), row), pl.BlockSpec((1, tm, MLA_KV_RANK), row),
                   pl.BlockSpec((1, tm, 2 * MLA_ROPE), row)],
        out_shape=[jax.ShapeDtypeStruct((B, S, MLA_Q_RANK), BF16),
                   jax.ShapeDtypeStruct((B, S, MLA_KV_RANK), BF16),
                   jax.ShapeDtypeStruct((B, S, 2 * MLA_ROPE), F32)],
        compiler_params=_params(2), name="mla_down")(x, gain, w_down_t, q_norm, kv_norm, cos_sin)


def _out_proj(o, w_out, x, mlp_gain, *, name):
    return _matmul(o, w_out[None], res=x, next_gain=mlp_gain, out_dtype=F32, tm=OUT_TM,
                   tn=w_out.shape[1], row_split=OUT_ROW_SPLIT, name=name)


def _sb_layer(x, gain, mlp_gain, w_in, w_out, j, w2_cast):
    B, S, _ = x.shape
    qkv = _matmul(x, w_in, j, gain=gain, out_dtype=BF16, name="sb_qkv")
    o, w_out_bf16, w2_bf16 = _sb_attention(qkv, B, S, [(w_out, j), w2_cast])
    return (*_out_proj(o, w_out_bf16, x, mlp_gain, name="sb_out"), w2_bf16)


def _fox_layer(x, gain, mlp_gain, w_in, b_f, q_gain, k_gain, w_out, j, w2_cast):
    B, S, _ = x.shape
    w_f = jnp.pad(w_in[j, :, 3 * ATTN_WIDTH:], ((0, 0), (0, LANES - N_HEADS))).astype(BF16)
    b_pad = jnp.pad(b_f, (0, LANES - N_HEADS))[None, :]
    qkv_gain = jnp.concatenate([jnp.tile(q_gain, N_HEADS), jnp.tile(k_gain, N_HEADS),
                                jnp.ones((ATTN_WIDTH,), F32)])[None, :]
    qkv = _matmul(x, jnp.swapaxes(w_in, 1, 2), j, w_rows=True, n_cols=3 * ATTN_WIDTH, gain=gain,
                  epi=_head_norm_epi, epi_tiles=2 * ATTN_WIDTH // MM_TN, extras=(qkv_gain,),
                  out_dtype=BF16, row_split=OUT_ROW_SPLIT, name="fox_qkv")
    cf_col, cf_rowT = _fox_gate(x, gain, w_f, b_pad)
    cf_row = cf_rowT[:, :N_HEADS, :].reshape(B, N_HEADS, 1, S)
    o, w_out_bf16, w2_bf16 = _fox_attention(qkv, cf_col, cf_row, B, S, [(w_out, j), w2_cast])
    return (*_out_proj(o, w_out_bf16, x, mlp_gain, name="fox_out"), w2_bf16)


def _mla_layer(x, positions, gain, mlp_gain, w_in, q_norm, kv_norm, w_uq, w_ukv, q_gain, k_gain,
               w_out_all, j, w2_cast):
    B, S, _ = x.shape
    r0 = MLA_Q_RANK + MLA_KV_RANK
    w_down = jnp.concatenate([w_in.T, _rot_columns(w_in[:, r0:]).T], axis=0).astype(BF16)
    w_q = w_uq.reshape(MLA_Q_RANK, N_HEADS, MLA_QK)
    w_q = jnp.concatenate([w_q, _rot_columns(w_q[:, :, MLA_NOPE:])], axis=2)
    w_q = w_q.reshape(MLA_Q_RANK, -1).astype(BF16)
    w_kv = w_ukv.reshape(MLA_KV_RANK, N_HEADS, MLA_NOPE + MLA_V)
    w_k = w_kv[:, :, :MLA_NOPE].reshape(MLA_KV_RANK, -1).astype(BF16)
    w_v = w_kv[:, :, MLA_NOPE:].reshape(MLA_KV_RANK, -1).astype(BF16)
    qg = jnp.tile(jnp.pad(q_gain, (0, MLA_QK_PAD - MLA_QK)), N_HEADS)[None, :]
    kg = jnp.tile(jnp.pad(k_gain, (0, MLA_QK_PAD - MLA_QK)), N_HEADS)[None, :]

    cos_sin = _rope_table(positions)
    c_q, c_kv, k_rope = _mla_down(x, gain, w_down, q_norm[None, :], kv_norm[None, :], cos_sin)
    q = _matmul(c_q, w_q[None], epi=_mla_q_epi, extras=(qg,), row_extras=(cos_sin,), out_dtype=BF16,
                row_split=EPI_ROW_SPLIT, name="mla_uq")
    k = _matmul(c_kv, w_k[None], epi=_mla_k_epi, extras=(kg,), row_extras=(k_rope,), out_dtype=BF16,
                out_tn=MM_TN * MLA_QK_PAD // MLA_NOPE, row_split=EPI_ROW_SPLIT, name="mla_uk")
    v = _matmul(c_kv, w_v[None], out_dtype=BF16, name="mla_uv")
    o, w_out_bf16, w2_bf16 = _mla_attention(q, k, v, B, S, [(w_out_all, j), w2_cast])
    return (*_out_proj(o, w_out_bf16, x, mlp_gain, name="mla_out"), w2_bf16)


def _mlp(x, h, w1, w2_bf16, i):
    a = _matmul(h, w1, i, epi=_relu2_epi, out_dtype=BF16, tm=MLP_UP_TM, name="mlp_up")
    return _matmul_ksplit(a, w2_bf16[None], 0, x, name="mlp_down")


def kernel(x, positions, mix_norm, mlp_norm, sb_w_in, sb_w_out, fox_w_in, fox_b_f, fox_q_gain,
           fox_k_gain, fox_w_out, mla_w_in, mla_q_norm, mla_kv_norm, mla_w_uq, mla_w_ukv,
           mla_q_gain, mla_k_gain, mla_w_out, mlp_w1, mlp_w2):
    depth = mix_norm.shape[0]
    for i in range(depth):
        kind, j = i % N_MIXERS, i // N_MIXERS
        gain = mix_norm[i][None, :]
        mlp_gain = mlp_norm[i][None, :]
        w2_cast = (mlp_w2, i)
        if kind == 0:
            x, h, w2_bf16 = _sb_layer(x, gain, mlp_gain, sb_w_in, sb_w_out, j, w2_cast)
        elif kind == 1:
            x, h, w2_bf16 = _fox_layer(x, gain, mlp_gain, fox_w_in, fox_b_f[j], fox_q_gain[j],
                                       fox_k_gain[j], fox_w_out, j, w2_cast)
        else:
            x, h, w2_bf16 = _mla_layer(x, positions, gain, mlp_gain, mla_w_in[j], mla_q_norm[j],
                                       mla_kv_norm[j], mla_w_uq[j], mla_w_ukv[j], mla_q_gain[j],
                                       mla_k_gain[j], mla_w_out, j, w2_cast)
        x = _mlp(x, h, mlp_w1, w2_bf16, i)
    return x
```

```python
import functools
import math

import jax
import jax.numpy as jnp
from jax import lax
from jax.experimental import pallas as pl
from jax.experimental.pallas import tpu as pltpu

F32 = jnp.float32
BF16 = jnp.bfloat16

D_MODEL = 2048
HEAD_DIM = 128
N_HEADS = D_MODEL // HEAD_DIM
ATTN_WIDTH = N_HEADS * HEAD_DIM
D_FF = 4 * D_MODEL
MLA_Q_RANK = 3 * D_MODEL // 8
MLA_KV_RANK = D_MODEL // 4
MLA_NOPE = 128
MLA_ROPE = 64
MLA_V = 128
MLA_QK = MLA_NOPE + MLA_ROPE
MLA_QK_PAD = 256
ROPE_THETA = 10000.0
EPS = 1e-6
N_MIXERS = 3
LOG2E = math.log2(math.e)

LANES = 128
ATTN_TQ = 256
ATTN_TK = 256
SCORE_LOOKAHEAD = 5
ONES_ROWS = 16
PREP_AHEAD = 4
SCAN_LOOKAHEAD = 2
MM_TM = 1024
MM_TN = 1024
MLP_UP_TM = 2048
MLP_TK = 4096
OUT_TM = 512
OUT_ROW_SPLIT = 2
EPI_ROW_SPLIT = 4
MM_VMEM_LIMIT = 56 * 1024 * 1024
ATTN_VMEM_LIMIT = 48 * 1024 * 1024

_NT = (((1,), (1,)), ((), ()))


def _params(n_grid, vmem_limit=None):
    return pltpu.CompilerParams(dimension_semantics=("arbitrary",) * n_grid,
                                vmem_limit_bytes=MM_VMEM_LIMIT if vmem_limit is None else vmem_limit)


def _rms(x):
    ms = jnp.mean(x * x, axis=-1, keepdims=True)
    return x * lax.rsqrt(ms + EPS)


def _rope_pair(pair, cos_sin):
    prod = pair * cos_sin
    return prod + pltpu.roll(prod, MLA_ROPE, axis=1)


def _rot_columns(w_rope):
    half = w_rope.shape[-1] // 2
    return jnp.concatenate([-w_rope[..., half:], w_rope[..., :half]], axis=-1)


def _mm_kernel(*refs, norm, w_rows, epi, epi_tiles, n_extra, n_row_extra, has_res, emit_norm, row_split):
    it = iter(refs)
    x_ref = next(it)
    g_ref = next(it) if norm else None
    w_ref = next(it)
    extra = [next(it) for _ in range(n_extra)]
    row_extra = [next(it) for _ in range(n_row_extra)]
    res_ref = next(it) if has_res else None
    gn_ref = next(it) if emit_norm else None
    o_ref = next(it)
    hn_ref = next(it) if emit_norm else None
    h_ref = next(it) if norm else None

    if norm:
        @pl.when(pl.program_id(2) == 0)
        def _():
            h_ref[...] = (_rms(x_ref[0]) * g_ref[...]).astype(BF16)

    w = w_ref[0].astype(BF16)
    sub = o_ref.shape[1] // row_split
    active = None if epi_tiles is None else pl.program_id(2) < epi_tiles

    def product(r):
        rows = slice(r * sub, (r + 1) * sub)
        h = h_ref[rows, :] if norm else x_ref[0, rows, :]
        if w_rows:
            return lax.dot_general(h, w, _NT, preferred_element_type=F32)
        return jnp.dot(h, w, preferred_element_type=F32)

    pending = [product(0)]
    for r in range(row_split):
        if r + 1 < row_split:
            pending.append(product(r + 1))
        rows = slice(r * sub, (r + 1) * sub)
        acc = pending.pop(0)
        if has_res:
            acc = res_ref[0, rows, :] + acc
        if epi is not None:
            acc = epi(acc, *extra, *[e[0, rows, :] for e in row_extra], active=active)
        o_ref[0, rows, :] = acc.astype(o_ref.dtype)
        if emit_norm:
            hn_ref[0, rows, :] = (_rms(acc) * gn_ref[...]).astype(BF16)


def _matmul(x, w, layer=0, *, w_rows=False, n_cols=None, gain=None, epi=None, epi_tiles=None, extras=(),
            row_extras=(), res=None, next_gain=None, out_dtype, tm=MM_TM, tn=MM_TN, out_tn=None,
            row_split=1, name):
    B, S, K = x.shape
    N = w.shape[1 if w_rows else 2] if n_cols is None else n_cols
    out_tn = tn if out_tn is None else out_tn
    n_out = N // tn * out_tn
    norm = gain is not None
    emit_norm = next_gain is not None
    assert not emit_norm or tn == N
    grid = (B, S // tm, N // tn)
    tile = lambda b, i, j: (b, i, j)
    in_specs = [pl.BlockSpec((1, tm, K), lambda b, i, j: (b, i, 0))]
    args = [x]
    if norm:
        in_specs.append(pl.BlockSpec((1, K), lambda b, i, j: (0, 0)))
        args.append(gain)
    if w_rows:
        in_specs.append(pl.BlockSpec((1, tn, K), lambda b, i, j: (layer, j, 0)))
    else:
        in_specs.append(pl.BlockSpec((1, K, tn), lambda b, i, j: (layer, 0, j)))
    args.append(w)
    for e in extras:
        in_specs.append(pl.BlockSpec((1, out_tn), lambda b, i, j: (0, j)))
        args.append(e)
    for e in row_extras:
        in_specs.append(pl.BlockSpec((1, tm, e.shape[2]), lambda b, i, j: (b, i, 0)))
        args.append(e)
    if res is not None:
        in_specs.append(pl.BlockSpec((1, tm, tn), tile))
        args.append(res)
    out_specs = pl.BlockSpec((1, tm, out_tn), tile)
    out_shape = jax.ShapeDtypeStruct((B, S, n_out), out_dtype)
    if emit_norm:
        in_specs.append(pl.BlockSpec((1, N), lambda b, i, j: (0, 0)))
        args.append(next_gain)
        out_specs = [out_specs, pl.BlockSpec((1, tm, tn), tile)]
        out_shape = [out_shape, jax.ShapeDtypeStruct((B, S, N), BF16)]
    body = functools.partial(_mm_kernel, norm=norm, w_rows=w_rows, epi=epi, epi_tiles=epi_tiles,
                             n_extra=len(extras), n_row_extra=len(row_extras),
                             has_res=res is not None, emit_norm=emit_norm, row_split=row_split)
    return pl.pallas_call(
        body, grid=grid, in_specs=in_specs, out_specs=out_specs, out_shape=out_shape,
        scratch_shapes=[pltpu.VMEM((tm, K), BF16)] if norm else [],
        compiler_params=_params(3), name=name)(*args)


def _mm_ksplit_kernel(x_ref, w_ref, res_ref, o_ref):
    @pl.when(pl.program_id(3) == 0)
    def _():
        o_ref[0] = res_ref[0]

    o_ref[0] += jnp.dot(x_ref[0], w_ref[0].astype(BF16), preferred_element_type=F32)


def _matmul_ksplit(x, w, layer, res, *, tm=MM_TM, tn=MM_TN, tk=MLP_TK, name):
    B, S, K = x.shape
    N = w.shape[2]
    return pl.pallas_call(
        _mm_ksplit_kernel, grid=(B, S // tm, N // tn, K // tk),
        in_specs=[pl.BlockSpec((1, tm, tk), lambda b, i, j, k: (b, i, k)),
                  pl.BlockSpec((1, tk, tn), lambda b, i, j, k: (layer, k, j)),
                  pl.BlockSpec((1, tm, tn), lambda b, i, j, k: (b, i, j))],
        out_specs=pl.BlockSpec((1, tm, tn), lambda b, i, j, k: (b, i, j)),
        out_shape=jax.ShapeDtypeStruct((B, S, N), F32),
        compiler_params=_params(4), name=name)(x, w, res)


def _relu2_epi(acc, active=None):
    a = jnp.maximum(acc, 0.0)
    return a * a


def _head_norm_epi(acc, g_ref, active=None):
    g = g_ref[...]
    outs = []
    for c in range(acc.shape[1] // HEAD_DIM):
        sl = slice(c * HEAD_DIM, (c + 1) * HEAD_DIM)
        blk = acc[:, sl]
        inv = lax.rsqrt(jnp.mean(blk * blk, axis=-1, keepdims=True) + EPS)
        if active is not None:
            inv = jnp.where(active, inv, 1.0)
        outs.append(blk * inv * g[:, sl])
    return jnp.concatenate(outs, axis=1)


def _mla_head_norm(nope, roped_twice, g):
    sq = nope * nope + 0.5 * (roped_twice * roped_twice)
    inv = lax.rsqrt(jnp.sum(sq, axis=-1, keepdims=True) * (1.0 / MLA_QK) + EPS)
    return jnp.concatenate([nope, roped_twice], axis=1) * inv * g


def _mla_q_epi(acc, g_ref, cos_sin, active=None):
    g = g_ref[...]
    outs = []
    for c in range(acc.shape[1] // MLA_QK_PAD):
        head = acc[:, c * MLA_QK_PAD:(c + 1) * MLA_QK_PAD]
        roped = _rope_pair(head[:, MLA_NOPE:], cos_sin)
        outs.append(_mla_head_norm(head[:, :MLA_NOPE], roped, g[:, c * MLA_QK_PAD:(c + 1) * MLA_QK_PAD]))
    return jnp.concatenate(outs, axis=1)


def _mla_k_epi(acc, g_ref, k_rope, active=None):
    g = g_ref[...]
    outs = []
    for c in range(acc.shape[1] // MLA_NOPE):
        outs.append(_mla_head_norm(acc[:, c * MLA_NOPE:(c + 1) * MLA_NOPE], k_rope,
                                   g[:, c * MLA_QK_PAD:(c + 1) * MLA_QK_PAD]))
    return jnp.concatenate(outs, axis=1)


def _causal_tiles(S):
    return [(qi, list(range(qi + 1))) for qi in range(S // ATTN_TQ)]


def _tile_masks():
    key = lax.broadcasted_iota(jnp.int32, (ATTN_TK, ATTN_TQ), 0)
    qry = lax.broadcasted_iota(jnp.int32, (ATTN_TK, ATTN_TQ), 1)
    return key, qry


def _raw_scores(k_ref, q_ref, qi, ki):
    return lax.dot_general(k_ref[ki * ATTN_TK:(ki + 1) * ATTN_TK, :],
                           q_ref[qi * ATTN_TQ:(qi + 1) * ATTN_TQ, :], _NT,
                           preferred_element_type=F32)


def _transpose_v_tile(vT_ref, v_tile, ki):
    vt = v_tile(ki).T.astype(BF16)
    vT_ref[0:vt.shape[0], ki * ATTN_TK:(ki + 1) * ATTN_TK] = vt


class _score_issuer:
    def __init__(self, k_ref, q_ref, prep, n_blocks):
        self.k_ref, self.q_ref, self.prep, self.n_blocks = k_ref, q_ref, prep, n_blocks
        self.ready = 0
        for _ in range(min(PREP_AHEAD, n_blocks)):
            self._prepare_next()

    def _prepare_next(self):
        self.prep(self.ready)
        self.ready += 1

    def prepare_ahead(self, qi):
        if qi + PREP_AHEAD < self.n_blocks:
            self._prepare_next()

    def __call__(self, tile):
        qi, ki = tile
        assert max(qi, ki) < self.ready, "score tile issued before its rows were prepared"
        return _raw_scores(self.k_ref, self.q_ref, qi, ki)


def _softmax_sweep(q_ref, k_ref, vT_ref, o_ref, *, prep, scale, bias, S):
    tq, tk = ATTN_TQ, ATTN_TK
    c = scale * LOG2E
    key, qry = _tile_masks()
    causal = key <= qry
    blocks = _causal_tiles(S)
    order = [(qi, ki) for qi, kis in blocks for ki in kis]
    dv = vT_ref.shape[0] - ONES_ROWS
    vT_ref[dv:, :] = jnp.ones((ONES_ROWS, S), BF16)
    issue = _score_issuer(k_ref, q_ref, prep, len(blocks))
    pending = [issue(t) for t in order[:SCORE_LOOKAHEAD]]
    issued = SCORE_LOOKAHEAD
    for qi, kis in blocks:
        issue.prepare_ahead(qi)
        m = acc = None
        for ki in kis:
            s_raw = pending.pop(0)
            if issued < len(order):
                pending.append(issue(order[issued]))
                issued += 1
            s = s_raw * c
            b = bias(ki, qi)
            if b is not None:
                s = s + b
            if ki == qi:
                s = jnp.where(causal, s, -jnp.inf)
            s_max = jnp.max(s, axis=0, keepdims=True)
            m_new = s_max if m is None else jnp.maximum(m, s_max)
            p = jnp.exp2(s - m_new)
            pv = jnp.dot(vT_ref[:, ki * tk:(ki + 1) * tk], p.astype(BF16),
                         preferred_element_type=F32)
            acc = pv if m is None else jnp.exp2(m - m_new) * acc + pv
            m = m_new
        o = acc[:dv] / acc[dv:dv + 1]
        o_ref[0, qi * tq:(qi + 1) * tq, :] = o.T.astype(o_ref.dtype)


def _sb_attn_kernel(q_ref, k_ref, v_ref, u_ref, o_ref, vT_ref, *, scale, S):
    tq, tk = ATTN_TQ, ATTN_TK
    c = scale * LOG2E
    key, qry = _tile_masks()
    strict = key < qry
    blocks = _causal_tiles(S)
    order = [(qi, ki) for qi, kis in blocks for ki in reversed(kis)]
    q2, k2 = q_ref.at[0], k_ref.at[0]

    def v_tile(ki):
        return v_ref[0, ki * tk:(ki + 1) * tk, :].astype(F32)

    def gate(t, z_raw):
        qi, ki = t
        z = z_raw * c
        soft = jnp.log2(1.0 + jnp.exp2(-jnp.abs(z)))
        nlk = jnp.maximum(z, 0.0) + soft
        if ki == qi:
            nlk = jnp.where(strict, nlk, 0.0)
        hi = nlk.astype(BF16)
        lo = (nlk - hi.astype(F32)).astype(BF16)
        scan = jnp.dot(u_ref[...], jnp.concatenate([hi, lo], axis=0),
                       preferred_element_type=F32)
        return z, scan

    issue = _score_issuer(k2, q2, functools.partial(_transpose_v_tile, vT_ref, v_tile), len(blocks))
    scored = [issue(t) for t in order[:SCORE_LOOKAHEAD]]
    gated = []
    n_scored = [SCORE_LOOKAHEAD]
    n_gated = [0]

    def advance():
        if n_scored[0] < len(order):
            scored.append(issue(order[n_scored[0]]))
            n_scored[0] += 1
        if n_gated[0] < len(order):
            gated.append(gate(order[n_gated[0]], scored.pop(0)))
            n_gated[0] += 1

    for _ in range(SCAN_LOOKAHEAD):
        advance()
    for qi, kis in blocks:
        issue.prepare_ahead(qi)
        suf = acc = None
        for ki in reversed(kis):
            z, scan = gated.pop(0)
            advance()
            x = z + scan
            if suf is not None:
                x = x + suf
            a = jnp.exp2(x)
            if ki == qi:
                a = jnp.where(strict, a, 0.0)
            pv = jnp.dot(vT_ref[:, ki * tk:(ki + 1) * tk], a.astype(BF16),
                         preferred_element_type=F32)
            acc = pv if acc is None else acc + pv
            col = scan[0:1, :]
            suf = col if suf is None else suf + col
        o_ref[0, qi * tq:(qi + 1) * tq, :] = acc.T.astype(o_ref.dtype)


def _attention_call(body, args, in_specs, out_spec, out_shape, scratch_shapes, side_casts, B, name):
    n_in, n_side = len(args), len(side_casts)
    steps = B * N_HEADS
    in_specs, out_specs, out_shapes = list(in_specs), [out_spec], [out_shape]
    for w, layer in side_casts:
        _, R, C = w.shape
        rows = R // steps
        in_specs.append(pl.BlockSpec((1, rows, C), lambda b, h, layer=layer: (layer, b * N_HEADS + h, 0)))
        out_specs.append(pl.BlockSpec((rows, C), lambda b, h: (b * N_HEADS + h, 0)))
        out_shapes.append(jax.ShapeDtypeStruct((R, C), BF16))

    def kernel_body(*refs):
        ins, side_in = refs[:n_in], refs[n_in:n_in + n_side]
        o_ref, side_out = refs[n_in + n_side], refs[n_in + n_side + 1:n_in + 2 * n_side + 1]
        for src, dst in zip(side_in, side_out):
            dst[...] = src[0].astype(BF16)
        body(*ins, o_ref, *refs[n_in + 2 * n_side + 1:])

    return pl.pallas_call(
        kernel_body, grid=(B, N_HEADS), in_specs=in_specs, out_specs=out_specs, out_shape=out_shapes,
        scratch_shapes=scratch_shapes, compiler_params=_params(2, ATTN_VMEM_LIMIT), name=name)(
            *args, *[w for w, _ in side_casts])


def _sb_attention(qkv, B, S, side_casts):
    tk = ATTN_TK
    upper = (jnp.arange(tk)[None, :] >= jnp.arange(tk)[:, None])
    neg_u2 = jnp.tile(jnp.where(upper, -1.0, 0.0), (1, 2)).astype(BF16)
    H = N_HEADS
    return _attention_call(
        functools.partial(_sb_attn_kernel, scale=1.0 / math.sqrt(HEAD_DIM), S=S),
        (qkv, qkv, qkv, neg_u2),
        [pl.BlockSpec((1, S, HEAD_DIM), lambda b, h: (b, 0, h)),
         pl.BlockSpec((1, S, HEAD_DIM), lambda b, h: (b, 0, H + h)),
         pl.BlockSpec((1, S, HEAD_DIM), lambda b, h: (b, 0, 2 * H + h)),
         pl.BlockSpec((tk, 2 * tk), lambda b, h: (0, 0))],
        pl.BlockSpec((1, S, HEAD_DIM), lambda b, h: (b, 0, h)),
        jax.ShapeDtypeStruct((B, S, ATTN_WIDTH), BF16),
        [pltpu.VMEM((HEAD_DIM, S), BF16)], side_casts, B, "sb_attention")


def _fox_attn_kernel(q_ref, k_ref, v_ref, cfc_ref, cfr_ref, o_ref, vT_ref, slab_ref, *, scale, S):
    h = pl.program_id(1)

    def v_tile(ki):
        return v_ref[0, ki * ATTN_TK:(ki + 1) * ATTN_TK, :].astype(F32)

    def prep(c):
        rows = slice(c * ATTN_TK, (c + 1) * ATTN_TK)
        blk = cfc_ref[0, rows, :]
        lane = lax.broadcasted_iota(jnp.int32, blk.shape, 1)
        col = jnp.sum(jnp.where(lane == h, blk, 0.0), axis=1, keepdims=True) * LOG2E
        slab_ref[rows, :] = jnp.broadcast_to(col, blk.shape)
        _transpose_v_tile(vT_ref, v_tile, c)

    def bias(ki, qi):
        cf_q = cfr_ref[0, 0, :, qi * ATTN_TQ:(qi + 1) * ATTN_TQ] * LOG2E
        cf_k = slab_ref[ki * ATTN_TK:(ki + 1) * ATTN_TK, :]
        return cf_q - jnp.concatenate([cf_k] * (ATTN_TQ // LANES), axis=1)

    _softmax_sweep(q_ref.at[0], k_ref.at[0], vT_ref, o_ref, prep=prep, scale=scale, bias=bias, S=S)


def _fox_attention(qkv, cf_col, cf_row, B, S, side_casts):
    H = N_HEADS
    return _attention_call(
        functools.partial(_fox_attn_kernel, scale=1.0 / math.sqrt(HEAD_DIM), S=S),
        (qkv, qkv, qkv, cf_col, cf_row),
        [pl.BlockSpec((1, S, HEAD_DIM), lambda b, h: (b, 0, h)),
         pl.BlockSpec((1, S, HEAD_DIM), lambda b, h: (b, 0, H + h)),
         pl.BlockSpec((1, S, HEAD_DIM), lambda b, h: (b, 0, 2 * H + h)),
         pl.BlockSpec((1, S, LANES), lambda b, h: (b, 0, 0)),
         pl.BlockSpec((1, 1, 1, S), lambda b, h: (b, h, 0, 0))],
        pl.BlockSpec((1, S, HEAD_DIM), lambda b, h: (b, 0, h)),
        jax.ShapeDtypeStruct((B, S, ATTN_WIDTH), BF16),
        [pltpu.VMEM((HEAD_DIM + ONES_ROWS, S), BF16), pltpu.VMEM((S, LANES), F32)],
        side_casts, B, "fox_attention")


def _mla_attn_kernel(q_ref, k_ref, v_ref, o_ref, vT_scr, *, scale, S):
    def v_tile(ki):
        return v_ref[0, ki * ATTN_TK:(ki + 1) * ATTN_TK, :].astype(F32)

    _softmax_sweep(q_ref.at[0], k_ref.at[0], vT_scr, o_ref,
                   prep=functools.partial(_transpose_v_tile, vT_scr, v_tile),
                   scale=scale, bias=lambda ki, qi: None, S=S)


def _mla_attention(q, k, v, B, S, side_casts):
    return _attention_call(
        functools.partial(_mla_attn_kernel, scale=1.0 / math.sqrt(MLA_QK), S=S),
        (q, k, v),
        [pl.BlockSpec((1, S, MLA_QK_PAD), lambda b, h: (b, 0, h)),
         pl.BlockSpec((1, S, MLA_QK_PAD), lambda b, h: (b, 0, h)),
         pl.BlockSpec((1, S, MLA_V), lambda b, h: (b, 0, h))],
        pl.BlockSpec((1, S, MLA_V), lambda b, h: (b, 0, h)),
        jax.ShapeDtypeStruct((B, S, N_HEADS * MLA_V), BF16),
        [pltpu.VMEM((MLA_V + ONES_ROWS, S), BF16)], side_casts, B, "mla_attention")


def _fox_gate_kernel(x_ref, g_ref, w_ref, b_ref, tri_ref, col_ref, row_ref, carry_ref):
    @pl.when(pl.program_id(1) == 0)
    def _():
        carry_ref[...] = jnp.zeros(carry_ref.shape, F32)

    h = (_rms(x_ref[0]) * g_ref[...]).astype(BF16)
    f = jnp.dot(h, w_ref[...], preferred_element_type=F32) + b_ref[...]
    log_f = jnp.minimum(f, 0.0) - jnp.log1p(jnp.exp(-jnp.abs(f)))
    p0 = log_f.astype(BF16)
    r1 = log_f - p0.astype(F32)
    p1 = r1.astype(BF16)
    p2 = (r1 - p1.astype(F32)).astype(BF16)
    tri = tri_ref[...]
    cs = (jnp.dot(tri, p0, preferred_element_type=F32) + jnp.dot(tri, p1, preferred_element_type=F32)
          + jnp.dot(tri, p2, preferred_element_type=F32)) + carry_ref[...]
    col_ref[0] = cs
    row_ref[0] = cs.T
    carry_ref[...] = cs[cs.shape[0] - 1:, :]


def _fox_gate(x, gain, w_f, b_f, ts=512):
    B, S, K = x.shape
    tri = (jnp.arange(ts)[None, :] <= jnp.arange(ts)[:, None]).astype(BF16)
    return pl.pallas_call(
        _fox_gate_kernel, grid=(B, S // ts),
        in_specs=[pl.BlockSpec((1, ts, K), lambda b, i: (b, i, 0)),
                  pl.BlockSpec((1, K), lambda b, i: (0, 0)),
                  pl.BlockSpec((K, LANES), lambda b, i: (0, 0)),
                  pl.BlockSpec((1, LANES), lambda b, i: (0, 0)),
                  pl.BlockSpec((ts, ts), lambda b, i: (0, 0))],
        out_specs=[pl.BlockSpec((1, ts, LANES), lambda b, i: (b, i, 0)),
                   pl.BlockSpec((1, LANES, ts), lambda b, i: (b, 0, i))],
        out_shape=[jax.ShapeDtypeStruct((B, S, LANES), F32), jax.ShapeDtypeStruct((B, LANES, S), F32)],
        scratch_shapes=[pltpu.VMEM((1, LANES), F32)],
        compiler_params=_params(2), name="fox_gate")(x, gain, w_f, b_f, tri)


def _rope_table_kernel(pos_ref, inv_ref, cos_sin_ref):
    ang = pos_ref[0].astype(F32) * inv_ref[...]
    cos_sin_ref[0] = jnp.concatenate([jnp.cos(ang), jnp.sin(ang)], axis=1)


def _rope_table(positions):
    B, S = positions.shape
    half = MLA_ROPE // 2
    inv_freq = ROPE_THETA ** (-jnp.arange(0, half, dtype=F32) * 2.0 / MLA_ROPE)
    inv2 = jnp.concatenate([inv_freq, inv_freq])[None, :]
    return pl.pallas_call(
        _rope_table_kernel, grid=(B,),
        in_specs=[pl.BlockSpec((1, S, 1), lambda b: (b, 0, 0)),
                  pl.BlockSpec((1, MLA_ROPE), lambda b: (0, 0))],
        out_specs=pl.BlockSpec((1, S, 2 * MLA_ROPE), lambda b: (b, 0, 0)),
        out_shape=jax.ShapeDtypeStruct((B, S, 2 * MLA_ROPE), F32),
        compiler_params=_params(1), name="rope_table")(positions.reshape(B, S, 1), inv2)


def _mla_down_kernel(x_ref, g_ref, w_ref, qn_ref, kvn_ref, cos_sin_ref, cq_ref, ckv_ref, kr_ref):
    h = (_rms(x_ref[0]) * g_ref[...]).astype(BF16)
    down = lax.dot_general(h, w_ref[...], _NT, preferred_element_type=F32)
    kv0 = MLA_Q_RANK
    r0 = MLA_Q_RANK + MLA_KV_RANK
    cq_ref[0] = (_rms(down[:, :kv0]) * qn_ref[...]).astype(BF16)
    ckv_ref[0] = (_rms(down[:, kv0:r0]) * kvn_ref[...]).astype(BF16)
    kr_ref[0] = _rope_pair(down[:, r0:r0 + 2 * MLA_ROPE], cos_sin_ref[0])


def _mla_down(x, gain, w_down_t, q_norm, kv_norm, cos_sin, tm=512):
    B, S, K = x.shape
    n_down = w_down_t.shape[0]
    row = lambda b, i: (b, i, 0)
    fixed = lambda b, i: (0, 0)
    return pl.pallas_call(
        _mla_down_kernel, grid=(B, S // tm),
        in_specs=[pl.BlockSpec((1, tm, K), row), pl.BlockSpec((1, K), fixed),
                  pl.BlockSpec((n_down, K), fixed),
                  pl.BlockSpec((1, MLA_Q_RANK), fixed), pl.BlockSpec((1, MLA_KV_RANK), fixed),
                  pl.BlockSpec((1, tm, 2 * MLA_ROPE), row)],
        out_specs=[pl.BlockSpec((1, tm, MLA_Q_RANK), row), pl.BlockSpec((1, tm, MLA_KV_RANK), row),
                   pl.BlockSpec((1, tm, 2 * MLA_ROPE), row)],
        out_shape=[jax.ShapeDtypeStruct((B, S, MLA_Q_RANK), BF16),
                   jax.ShapeDtypeStruct((B, S, MLA_KV_RANK), BF16),
                   jax.ShapeDtypeStruct((B, S, 2 * MLA_ROPE), F32)],
        compiler_params=_params(2), name="mla_down")(x, gain, w_down_t, q_norm, kv_norm, cos_sin)


def _out_proj(o, w_out, x, mlp_gain, *, name):
    return _matmul(o, w_out[None], res=x, next_gain=mlp_gain, out_dtype=F32, tm=OUT_TM,
                   tn=w_out.shape[1], row_split=OUT_ROW_SPLIT, name=name)


def _sb_layer(x, gain, mlp_gain, w_in, w_out, j, w2_cast):
    B, S, _ = x.shape
    qkv = _matmul(x, w_in, j, gain=gain, out_dtype=BF16, name="sb_qkv")
    o, w_out_bf16, w2_bf16 = _sb_attention(qkv, B, S, [(w_out, j), w2_cast])
    return (*_out_proj(o, w_out_bf16, x, mlp_gain, name="sb_out"), w2_bf16)


def _fox_layer(x, gain, mlp_gain, w_in, b_f, q_gain, k_gain, w_out, j, w2_cast):
    B, S, _ = x.shape
    w_f = jnp.pad(w_in[j, :, 3 * ATTN_WIDTH:], ((0, 0), (0, LANES - N_HEADS))).astype(BF16)
    b_pad = jnp.pad(b_f, (0, LANES - N_HEADS))[None, :]
    qkv_gain = jnp.concatenate([jnp.tile(q_gain, N_HEADS), jnp.tile(k_gain, N_HEADS),
                                jnp.ones((ATTN_WIDTH,), F32)])[None, :]
    qkv = _matmul(x, jnp.swapaxes(w_in, 1, 2), j, w_rows=True, n_cols=3 * ATTN_WIDTH, gain=gain,
                  epi=_head_norm_epi, epi_tiles=2 * ATTN_WIDTH // MM_TN, extras=(qkv_gain,),
                  out_dtype=BF16, row_split=OUT_ROW_SPLIT, name="fox_qkv")
    cf_col, cf_rowT = _fox_gate(x, gain, w_f, b_pad)
    cf_row = cf_rowT[:, :N_HEADS, :].reshape(B, N_HEADS, 1, S)
    o, w_out_bf16, w2_bf16 = _fox_attention(qkv, cf_col, cf_row, B, S, [(w_out, j), w2_cast])
    return (*_out_proj(o, w_out_bf16, x, mlp_gain, name="fox_out"), w2_bf16)


def _mla_layer(x, positions, gain, mlp_gain, w_in, q_norm, kv_norm, w_uq, w_ukv, q_gain, k_gain,
               w_out_all, j, w2_cast):
    B, S, _ = x.shape
    r0 = MLA_Q_RANK + MLA_KV_RANK
    w_down = jnp.concatenate([w_in.T, _rot_columns(w_in[:, r0:]).T], axis=0).astype(BF16)
    w_q = w_uq.reshape(MLA_Q_RANK, N_HEADS, MLA_QK)
    w_q = jnp.concatenate([w_q, _rot_columns(w_q[:, :, MLA_NOPE:])], axis=2)
    w_q = w_q.reshape(MLA_Q_RANK, -1).astype(BF16)
    w_kv = w_ukv.reshape(MLA_KV_RANK, N_HEADS, MLA_NOPE + MLA_V)
    w_k = w_kv[:, :, :MLA_NOPE].reshape(MLA_KV_RANK, -1).astype(BF16)
    w_v = w_kv[:, :, MLA_NOPE:].reshape(MLA_KV_RANK, -1).astype(BF16)
    qg = jnp.tile(jnp.pad(q_gain, (0, MLA_QK_PAD - MLA_QK)), N_HEADS)[None, :]
    kg = jnp.tile(jnp.pad(k_gain, (0, MLA_QK_PAD - MLA_QK)), N_HEADS)[None, :]

    cos_sin = _rope_table(positions)
    c_q, c_kv, k_rope = _mla_down(x, gain, w_down, q_norm[None, :], kv_norm[None, :], cos_sin)
    q = _matmul(c_q, w_q[None], epi=_mla_q_epi, extras=(qg,), row_extras=(cos_sin,), out_dtype=BF16,
                row_split=EPI_ROW_SPLIT, name="mla_uq")
    k = _matmul(c_kv, w_k[None], epi=_mla_k_epi, extras=(kg,), row_extras=(k_rope,), out_dtype=BF16,
                out_tn=MM_TN * MLA_QK_PAD // MLA_NOPE, row_split=EPI_ROW_SPLIT, name="mla_uk")
    v = _matmul(c_kv, w_v[None], out_dtype=BF16, name="mla_uv")
    o, w_out_bf16, w2_bf16 = _mla_attention(q, k, v, B, S, [(w_out_all, j), w2_cast])
    return (*_out_proj(o, w_out_bf16, x, mlp_gain, name="mla_out"), w2_bf16)


def _mlp(x, h, w1, w2_bf16, i):
    a = _matmul(h, w1, i, epi=_relu2_epi, out_dtype=BF16, tm=MLP_UP_TM, name="mlp_up")
    return _matmul_ksplit(a, w2_bf16[None], 0, x, name="mlp_down")


def kernel(x, positions, mix_norm, mlp_norm, sb_w_in, sb_w_out, fox_w_in, fox_b_f, fox_q_gain,
           fox_k_gain, fox_w_out, mla_w_in, mla_q_norm, mla_kv_norm, mla_w_uq, mla_w_ukv,
           mla_q_gain, mla_k_gain, mla_w_out, mlp_w1, mlp_w2):
    depth = mix_norm.shape[0]
    for i in range(depth):
        kind, j = i % N_MIXERS, i // N_MIXERS
        gain = mix_norm[i][None, :]
        mlp_gain = mlp_norm[i][None, :]
        w2_cast = (mlp_w2, i)
        if kind == 0:
            x, h, w2_bf16 = _sb_layer(x, gain, mlp_gain, sb_w_in, sb_w_out, j, w2_cast)
        elif kind == 1:
            x, h, w2_bf16 = _fox_layer(x, gain, mlp_gain, fox_w_in, fox_b_f[j], fox_q_gain[j],
                                       fox_k_gain[j], fox_w_out, j, w2_cast)
        else:
            x, h, w2_bf16 = _mla_layer(x, positions, gain, mlp_gain, mla_w_in[j], mla_q_norm[j],
                                       mla_kv_norm[j], mla_w_uq[j], mla_w_ukv[j], mla_q_gain[j],
                                       mla_k_gain[j], mla_w_out, j, w2_cast)
        x = _mlp(x, h, mlp_w1, w2_bf16, i)
    return x
```

```python
import functools
import math

import jax
import jax.numpy as jnp
from jax import lax
from jax.experimental import pallas as pl
from jax.experimental.pallas import tpu as pltpu

F32 = jnp.float32
BF16 = jnp.bfloat16

D_MODEL = 2048
HEAD_DIM = 128
N_HEADS = D_MODEL // HEAD_DIM
ATTN_WIDTH = N_HEADS * HEAD_DIM
D_FF = 4 * D_MODEL
MLA_Q_RANK = 3 * D_MODEL // 8
MLA_KV_RANK = D_MODEL // 4
MLA_NOPE = 128
MLA_ROPE = 64
MLA_V = 128
MLA_QK = MLA_NOPE + MLA_ROPE
MLA_QK_PAD = 256
ROPE_THETA = 10000.0
EPS = 1e-6
N_MIXERS = 3
LOG2E = math.log2(math.e)

LANES = 128
ATTN_TQ = 256
ATTN_TK = 256
SCORE_LOOKAHEAD = 5
ONES_ROWS = 16
PREP_AHEAD = 4
SCAN_LOOKAHEAD = 2
MM_TM = 1024
MM_TN = 1024
MLP_UP_TM = 2048
MLP_TK = 4096
OUT_TM = 512
OUT_ROW_SPLIT = 2
EPI_ROW_SPLIT = 4
MM_VMEM_LIMIT = 56 * 1024 * 1024
ATTN_VMEM_LIMIT = 48 * 1024 * 1024

_NT = (((1,), (1,)), ((), ()))


def _params(n_grid, vmem_limit=None):
    return pltpu.CompilerParams(dimension_semantics=("arbitrary",) * n_grid,
                                vmem_limit_bytes=MM_VMEM_LIMIT if vmem_limit is None else vmem_limit)


def _rms(x):
    ms = jnp.mean(x * x, axis=-1, keepdims=True)
    return x * lax.rsqrt(ms + EPS)


def _rope_pair(pair, cos_sin):
    prod = pair * cos_sin
    return prod + pltpu.roll(prod, MLA_ROPE, axis=1)


def _rot_columns(w_rope):
    half = w_rope.shape[-1] // 2
    return jnp.concatenate([-w_rope[..., half:], w_rope[..., :half]], axis=-1)


def _mm_kernel(*refs, norm, w_rows, epi, epi_tiles, n_extra, n_row_extra, has_res, emit_norm, side,
               row_split):
    it = iter(refs)
    x_ref = next(it)
    g_ref = next(it) if norm else None
    w_ref = next(it)
    extra = [next(it) for _ in range(n_extra)]
    row_extra = [next(it) for _ in range(n_row_extra)]
    res_ref = next(it) if has_res else None
    gn_ref = next(it) if emit_norm else None
    sw_ref = next(it) if side else None
    o_ref = next(it)
    hn_ref = next(it) if emit_norm else None
    so_ref = next(it) if side else None
    h_ref = next(it) if norm else None

    if norm:
        @pl.when(pl.program_id(2) == 0)
        def _():
            h_ref[...] = (_rms(x_ref[0]) * g_ref[...]).astype(BF16)
            if side:
                so_ref[0] = jnp.dot(h_ref[...], sw_ref[...], preferred_element_type=F32)

    w = w_ref[0].astype(BF16)
    sub = o_ref.shape[1] // row_split
    active = None if epi_tiles is None else pl.program_id(2) < epi_tiles

    def product(r):
        rows = slice(r * sub, (r + 1) * sub)
        h = h_ref[rows, :] if norm else x_ref[0, rows, :]
        if w_rows:
            return lax.dot_general(h, w, _NT, preferred_element_type=F32)
        return jnp.dot(h, w, preferred_element_type=F32)

    pending = [product(0)]
    for r in range(row_split):
        if r + 1 < row_split:
            pending.append(product(r + 1))
        rows = slice(r * sub, (r + 1) * sub)
        acc = pending.pop(0)
        if has_res:
            acc = res_ref[0, rows, :] + acc
        if epi is not None:
            acc = epi(acc, *extra, *[e[0, rows, :] for e in row_extra], active=active)
        o_ref[0, rows, :] = acc.astype(o_ref.dtype)
        if emit_norm:
            hn_ref[0, rows, :] = (_rms(acc) * gn_ref[...]).astype(BF16)


def _matmul(x, w, layer=0, *, w_rows=False, n_cols=None, gain=None, epi=None, epi_tiles=None, extras=(),
            row_extras=(), res=None, next_gain=None, side_w=None, out_dtype, tm=MM_TM, tn=MM_TN,
            out_tn=None, row_split=1, name):
    B, S, K = x.shape
    N = w.shape[1 if w_rows else 2] if n_cols is None else n_cols
    out_tn = tn if out_tn is None else out_tn
    n_out = N // tn * out_tn
    norm = gain is not None
    emit_norm = next_gain is not None
    assert not emit_norm or tn == N
    grid = (B, S // tm, N // tn)
    tile = lambda b, i, j: (b, i, j)
    in_specs = [pl.BlockSpec((1, tm, K), lambda b, i, j: (b, i, 0))]
    args = [x]
    if norm:
        in_specs.append(pl.BlockSpec((1, K), lambda b, i, j: (0, 0)))
        args.append(gain)
    if w_rows:
        in_specs.append(pl.BlockSpec((1, tn, K), lambda b, i, j: (layer, j, 0)))
    else:
        in_specs.append(pl.BlockSpec((1, K, tn), lambda b, i, j: (layer, 0, j)))
    args.append(w)
    for e in extras:
        in_specs.append(pl.BlockSpec((1, out_tn), lambda b, i, j: (0, j)))
        args.append(e)
    for e in row_extras:
        in_specs.append(pl.BlockSpec((1, tm, e.shape[2]), lambda b, i, j: (b, i, 0)))
        args.append(e)
    if res is not None:
        in_specs.append(pl.BlockSpec((1, tm, tn), tile))
        args.append(res)
    out_specs = pl.BlockSpec((1, tm, out_tn), tile)
    out_shape = jax.ShapeDtypeStruct((B, S, n_out), out_dtype)
    if emit_norm:
        in_specs.append(pl.BlockSpec((1, N), lambda b, i, j: (0, 0)))
        args.append(next_gain)
        out_specs = [out_specs, pl.BlockSpec((1, tm, tn), tile)]
        out_shape = [out_shape, jax.ShapeDtypeStruct((B, S, N), BF16)]
    side = side_w is not None
    if side:
        assert norm and not emit_norm
        width = side_w.shape[1]
        in_specs.append(pl.BlockSpec((K, width), lambda b, i, j: (0, 0)))
        args.append(side_w)
        out_specs = [out_specs, pl.BlockSpec((1, tm, width), lambda b, i, j: (b, i, 0))]
        out_shape = [out_shape, jax.ShapeDtypeStruct((B, S, width), F32)]
    body = functools.partial(_mm_kernel, norm=norm, w_rows=w_rows, epi=epi, epi_tiles=epi_tiles,
                             n_extra=len(extras), n_row_extra=len(row_extras),
                             has_res=res is not None, emit_norm=emit_norm, side=side,
                             row_split=row_split)
    return pl.pallas_call(
        body, grid=grid, in_specs=in_specs, out_specs=out_specs, out_shape=out_shape,
        scratch_shapes=[pltpu.VMEM((tm, K), BF16)] if norm else [],
        compiler_params=_params(3), name=name)(*args)


def _mm_ksplit_kernel(x_ref, w_ref, res_ref, o_ref):
    @pl.when(pl.program_id(3) == 0)
    def _():
        o_ref[0] = res_ref[0]

    o_ref[0] += jnp.dot(x_ref[0], w_ref[0].astype(BF16), preferred_element_type=F32)


def _matmul_ksplit(x, w, layer, res, *, tm=MM_TM, tn=MM_TN, tk=MLP_TK, name):
    B, S, K = x.shape
    N = w.shape[2]
    return pl.pallas_call(
        _mm_ksplit_kernel, grid=(B, S // tm, N // tn, K // tk),
        in_specs=[pl.BlockSpec((1, tm, tk), lambda b, i, j, k: (b, i, k)),
                  pl.BlockSpec((1, tk, tn), lambda b, i, j, k: (layer, k, j)),
                  pl.BlockSpec((1, tm, tn), lambda b, i, j, k: (b, i, j))],
        out_specs=pl.BlockSpec((1, tm, tn), lambda b, i, j, k: (b, i, j)),
        out_shape=jax.ShapeDtypeStruct((B, S, N), F32),
        compiler_params=_params(4), name=name)(x, w, res)


def _relu2_epi(acc, active=None):
    a = jnp.maximum(acc, 0.0)
    return a * a


def _head_norm_epi(acc, g_ref, active=None):
    g = g_ref[...]
    outs = []
    for c in range(acc.shape[1] // HEAD_DIM):
        sl = slice(c * HEAD_DIM, (c + 1) * HEAD_DIM)
        blk = acc[:, sl]
        inv = lax.rsqrt(jnp.mean(blk * blk, axis=-1, keepdims=True) + EPS)
        if active is not None:
            inv = jnp.where(active, inv, 1.0)
        outs.append(blk * inv * g[:, sl])
    return jnp.concatenate(outs, axis=1)


def _mla_head_norm(nope, roped_twice, g):
    sq = nope * nope + 0.5 * (roped_twice * roped_twice)
    inv = lax.rsqrt(jnp.sum(sq, axis=-1, keepdims=True) * (1.0 / MLA_QK) + EPS)
    return jnp.concatenate([nope, roped_twice], axis=1) * inv * g


def _mla_q_epi(acc, g_ref, cos_sin, active=None):
    g = g_ref[...]
    outs = []
    for c in range(acc.shape[1] // MLA_QK_PAD):
        head = acc[:, c * MLA_QK_PAD:(c + 1) * MLA_QK_PAD]
        roped = _rope_pair(head[:, MLA_NOPE:], cos_sin)
        outs.append(_mla_head_norm(head[:, :MLA_NOPE], roped, g[:, c * MLA_QK_PAD:(c + 1) * MLA_QK_PAD]))
    return jnp.concatenate(outs, axis=1)


def _mla_k_epi(acc, g_ref, k_rope, active=None):
    g = g_ref[...]
    outs = []
    for c in range(acc.shape[1] // MLA_NOPE):
        outs.append(_mla_head_norm(acc[:, c * MLA_NOPE:(c + 1) * MLA_NOPE], k_rope,
                                   g[:, c * MLA_QK_PAD:(c + 1) * MLA_QK_PAD]))
    return jnp.concatenate(outs, axis=1)


def _causal_tiles(S):
    return [(qi, list(range(qi + 1))) for qi in range(S // ATTN_TQ)]


def _tile_masks():
    key = lax.broadcasted_iota(jnp.int32, (ATTN_TK, ATTN_TQ), 0)
    qry = lax.broadcasted_iota(jnp.int32, (ATTN_TK, ATTN_TQ), 1)
    return key, qry


def _raw_scores(k_ref, q_ref, qi, ki):
    return lax.dot_general(k_ref[ki * ATTN_TK:(ki + 1) * ATTN_TK, :],
                           q_ref[qi * ATTN_TQ:(qi + 1) * ATTN_TQ, :], _NT,
                           preferred_element_type=F32)


def _transpose_v_tile(vT_ref, v_tile, ki):
    vt = v_tile(ki).T.astype(BF16)
    vT_ref[0:vt.shape[0], ki * ATTN_TK:(ki + 1) * ATTN_TK] = vt


class _score_issuer:
    def __init__(self, k_ref, q_ref, prep, n_blocks):
        self.k_ref, self.q_ref, self.prep, self.n_blocks = k_ref, q_ref, prep, n_blocks
        self.ready = 0
        for _ in range(min(PREP_AHEAD, n_blocks)):
            self._prepare_next()

    def _prepare_next(self):
        self.prep(self.ready)
        self.ready += 1

    def prepare_ahead(self, qi):
        if qi + PREP_AHEAD < self.n_blocks:
            self._prepare_next()

    def __call__(self, tile):
        qi, ki = tile
        assert max(qi, ki) < self.ready, "score tile issued before its rows were prepared"
        return _raw_scores(self.k_ref, self.q_ref, qi, ki)


def _softmax_sweep(q_ref, k_ref, vT_ref, o_ref, *, prep, scale, bias, S):
    tq, tk = ATTN_TQ, ATTN_TK
    c = scale * LOG2E
    key, qry = _tile_masks()
    causal = key <= qry
    blocks = _causal_tiles(S)
    order = [(qi, ki) for qi, kis in blocks for ki in kis]
    dv = vT_ref.shape[0] - ONES_ROWS
    vT_ref[dv:, :] = jnp.ones((ONES_ROWS, S), BF16)
    issue = _score_issuer(k_ref, q_ref, prep, len(blocks))
    pending = [issue(t) for t in order[:SCORE_LOOKAHEAD]]
    issued = SCORE_LOOKAHEAD
    for qi, kis in blocks:
        issue.prepare_ahead(qi)
        m = acc = None
        for ki in kis:
            s_raw = pending.pop(0)
            if issued < len(order):
                pending.append(issue(order[issued]))
                issued += 1
            s = s_raw * c
            b = bias(ki, qi)
            if b is not None:
                s = s + b
            if ki == qi:
                s = jnp.where(causal, s, -jnp.inf)
            s_max = jnp.max(s, axis=0, keepdims=True)
            m_new = s_max if m is None else jnp.maximum(m, s_max)
            p = jnp.exp2(s - m_new)
            pv = jnp.dot(vT_ref[:, ki * tk:(ki + 1) * tk], p.astype(BF16),
                         preferred_element_type=F32)
            acc = pv if m is None else jnp.exp2(m - m_new) * acc + pv
            m = m_new
        o = acc[:dv] / acc[dv:dv + 1]
        o_ref[0, qi * tq:(qi + 1) * tq, :] = o.T.astype(o_ref.dtype)


def _sb_attn_kernel(q_ref, k_ref, v_ref, u_ref, o_ref, vT_ref, *, scale, S):
    tq, tk = ATTN_TQ, ATTN_TK
    c = scale * LOG2E
    key, qry = _tile_masks()
    strict = key < qry
    blocks = _causal_tiles(S)
    order = [(qi, ki) for qi, kis in blocks for ki in reversed(kis)]
    q2, k2 = q_ref.at[0], k_ref.at[0]

    def v_tile(ki):
        return v_ref[0, ki * tk:(ki + 1) * tk, :].astype(F32)

    def gate(t, z_raw):
        qi, ki = t
        z = z_raw * c
        soft = jnp.log2(1.0 + jnp.exp2(-jnp.abs(z)))
        nlk = jnp.maximum(z, 0.0) + soft
        if ki == qi:
            nlk = jnp.where(strict, nlk, 0.0)
        hi = nlk.astype(BF16)
        lo = (nlk - hi.astype(F32)).astype(BF16)
        scan = jnp.dot(u_ref[...], jnp.concatenate([hi, lo], axis=0),
                       preferred_element_type=F32)
        return z, scan

    issue = _score_issuer(k2, q2, functools.partial(_transpose_v_tile, vT_ref, v_tile), len(blocks))
    scored = [issue(t) for t in order[:SCORE_LOOKAHEAD]]
    gated = []
    n_scored = [SCORE_LOOKAHEAD]
    n_gated = [0]

    def advance():
        if n_scored[0] < len(order):
            scored.append(issue(order[n_scored[0]]))
            n_scored[0] += 1
        if n_gated[0] < len(order):
            gated.append(gate(order[n_gated[0]], scored.pop(0)))
            n_gated[0] += 1

    for _ in range(SCAN_LOOKAHEAD):
        advance()
    for qi, kis in blocks:
        issue.prepare_ahead(qi)
        suf = acc = None
        for ki in reversed(kis):
            z, scan = gated.pop(0)
            advance()
            x = z + scan
            if suf is not None:
                x = x + suf
            a = jnp.exp2(x)
            if ki == qi:
                a = jnp.where(strict, a, 0.0)
            pv = jnp.dot(vT_ref[:, ki * tk:(ki + 1) * tk], a.astype(BF16),
                         preferred_element_type=F32)
            acc = pv if acc is None else acc + pv
            col = scan[0:1, :]
            suf = col if suf is None else suf + col
        o_ref[0, qi * tq:(qi + 1) * tq, :] = acc.T.astype(o_ref.dtype)


def _attention_call(body, args, in_specs, out_spec, out_shape, scratch_shapes, side_casts, B, name):
    n_in, n_side = len(args), len(side_casts)
    steps = B * N_HEADS
    in_specs, out_specs, out_shapes = list(in_specs), [out_spec], [out_shape]
    for w, layer in side_casts:
        _, R, C = w.shape
        rows = R // steps
        in_specs.append(pl.BlockSpec((1, rows, C), lambda b, h, layer=layer: (layer, b * N_HEADS + h, 0)))
        out_specs.append(pl.BlockSpec((rows, C), lambda b, h: (b * N_HEADS + h, 0)))
        out_shapes.append(jax.ShapeDtypeStruct((R, C), BF16))

    def kernel_body(*refs):
        ins, side_in = refs[:n_in], refs[n_in:n_in + n_side]
        o_ref, side_out = refs[n_in + n_side], refs[n_in + n_side + 1:n_in + 2 * n_side + 1]
        for src, dst in zip(side_in, side_out):
            dst[...] = src[0].astype(BF16)
        body(*ins, o_ref, *refs[n_in + 2 * n_side + 1:])

    return pl.pallas_call(
        kernel_body, grid=(B, N_HEADS), in_specs=in_specs, out_specs=out_specs, out_shape=out_shapes,
        scratch_shapes=scratch_shapes, compiler_params=_params(2, ATTN_VMEM_LIMIT), name=name)(
            *args, *[w for w, _ in side_casts])


def _sb_attention(qkv, B, S, side_casts):
    tk = ATTN_TK
    upper = (jnp.arange(tk)[None, :] >= jnp.arange(tk)[:, None])
    neg_u2 = jnp.tile(jnp.where(upper, -1.0, 0.0), (1, 2)).astype(BF16)
    H = N_HEADS
    return _attention_call(
        functools.partial(_sb_attn_kernel, scale=1.0 / math.sqrt(HEAD_DIM), S=S),
        (qkv, qkv, qkv, neg_u2),
        [pl.BlockSpec((1, S, HEAD_DIM), lambda b, h: (b, 0, h)),
         pl.BlockSpec((1, S, HEAD_DIM), lambda b, h: (b, 0, H + h)),
         pl.BlockSpec((1, S, HEAD_DIM), lambda b, h: (b, 0, 2 * H + h)),
         pl.BlockSpec((tk, 2 * tk), lambda b, h: (0, 0))],
        pl.BlockSpec((1, S, HEAD_DIM), lambda b, h: (b, 0, h)),
        jax.ShapeDtypeStruct((B, S, ATTN_WIDTH), BF16),
        [pltpu.VMEM((HEAD_DIM, S), BF16)], side_casts, B, "sb_attention")


def _fox_attn_kernel(q_ref, k_ref, v_ref, cfc_ref, cfr_ref, o_ref, vT_ref, slab_ref, *, scale, S):
    h = pl.program_id(1)

    def v_tile(ki):
        return v_ref[0, ki * ATTN_TK:(ki + 1) * ATTN_TK, :].astype(F32)

    def prep(c):
        rows = slice(c * ATTN_TK, (c + 1) * ATTN_TK)
        blk = cfc_ref[0, rows, :]
        lane = lax.broadcasted_iota(jnp.int32, blk.shape, 1)
        col = jnp.sum(jnp.where(lane == h, blk, 0.0), axis=1, keepdims=True) * LOG2E
        slab_ref[rows, :] = jnp.broadcast_to(col, blk.shape)
        _transpose_v_tile(vT_ref, v_tile, c)

    def bias(ki, qi):
        cf_q = cfr_ref[0, 0, :, qi * ATTN_TQ:(qi + 1) * ATTN_TQ] * LOG2E
        cf_k = slab_ref[ki * ATTN_TK:(ki + 1) * ATTN_TK, :]
        return cf_q - jnp.concatenate([cf_k] * (ATTN_TQ // LANES), axis=1)

    _softmax_sweep(q_ref.at[0], k_ref.at[0], vT_ref, o_ref, prep=prep, scale=scale, bias=bias, S=S)


def _fox_attention(qkv, cf_col, cf_row, B, S, side_casts):
    H = N_HEADS
    return _attention_call(
        functools.partial(_fox_attn_kernel, scale=1.0 / math.sqrt(HEAD_DIM), S=S),
        (qkv, qkv, qkv, cf_col, cf_row),
        [pl.BlockSpec((1, S, HEAD_DIM), lambda b, h: (b, 0, h)),
         pl.BlockSpec((1, S, HEAD_DIM), lambda b, h: (b, 0, H + h)),
         pl.BlockSpec((1, S, HEAD_DIM), lambda b, h: (b, 0, 2 * H + h)),
         pl.BlockSpec((1, S, LANES), lambda b, h: (b, 0, 0)),
         pl.BlockSpec((1, 1, 1, S), lambda b, h: (b, h, 0, 0))],
        pl.BlockSpec((1, S, HEAD_DIM), lambda b, h: (b, 0, h)),
        jax.ShapeDtypeStruct((B, S, ATTN_WIDTH), BF16),
        [pltpu.VMEM((HEAD_DIM + ONES_ROWS, S), BF16), pltpu.VMEM((S, LANES), F32)],
        side_casts, B, "fox_attention")


def _mla_attn_kernel(q_ref, k_ref, v_ref, o_ref, vT_scr, *, scale, S):
    def v_tile(ki):
        return v_ref[0, ki * ATTN_TK:(ki + 1) * ATTN_TK, :].astype(F32)

    _softmax_sweep(q_ref.at[0], k_ref.at[0], vT_scr, o_ref,
                   prep=functools.partial(_transpose_v_tile, vT_scr, v_tile),
                   scale=scale, bias=lambda ki, qi: None, S=S)


def _mla_attention(q, k, v, B, S, side_casts):
    return _attention_call(
        functools.partial(_mla_attn_kernel, scale=1.0 / math.sqrt(MLA_QK), S=S),
        (q, k, v),
        [pl.BlockSpec((1, S, MLA_QK_PAD), lambda b, h: (b, 0, h)),
         pl.BlockSpec((1, S, MLA_QK_PAD), lambda b, h: (b, 0, h)),
         pl.BlockSpec((1, S, MLA_V), lambda b, h: (b, 0, h))],
        pl.BlockSpec((1, S, MLA_V), lambda b, h: (b, 0, h)),
        jax.ShapeDtypeStruct((B, S, N_HEADS * MLA_V), BF16),
        [pltpu.VMEM((MLA_V + ONES_ROWS, S), BF16)], side_casts, B, "mla_attention")


def _fox_gate_kernel(f_ref, b_ref, tri_ref, col_ref, row_ref, carry_ref):
    @pl.when(pl.program_id(1) == 0)
    def _():
        carry_ref[...] = jnp.zeros(carry_ref.shape, F32)

    f = f_ref[0] + b_ref[...]
    log_f = jnp.minimum(f, 0.0) - jnp.log1p(jnp.exp(-jnp.abs(f)))
    p0 = log_f.astype(BF16)
    r1 = log_f - p0.astype(F32)
    p1 = r1.astype(BF16)
    p2 = (r1 - p1.astype(F32)).astype(BF16)
    tri = tri_ref[...]
    cs = (jnp.dot(tri, p0, preferred_element_type=F32) + jnp.dot(tri, p1, preferred_element_type=F32)
          + jnp.dot(tri, p2, preferred_element_type=F32)) + carry_ref[...]
    col_ref[0] = cs
    row_ref[0] = cs.T
    carry_ref[...] = cs[cs.shape[0] - 1:, :]


def _fox_gate(f, b_f, ts=512):
    B, S, _ = f.shape
    tri = (jnp.arange(ts)[None, :] <= jnp.arange(ts)[:, None]).astype(BF16)
    return pl.pallas_call(
        _fox_gate_kernel, grid=(B, S // ts),
        in_specs=[pl.BlockSpec((1, ts, LANES), lambda b, i: (b, i, 0)),
                  pl.BlockSpec((1, LANES), lambda b, i: (0, 0)),
                  pl.BlockSpec((ts, ts), lambda b, i: (0, 0))],
        out_specs=[pl.BlockSpec((1, ts, LANES), lambda b, i: (b, i, 0)),
                   pl.BlockSpec((1, LANES, ts), lambda b, i: (b, 0, i))],
        out_shape=[jax.ShapeDtypeStruct((B, S, LANES), F32), jax.ShapeDtypeStruct((B, LANES, S), F32)],
        scratch_shapes=[pltpu.VMEM((1, LANES), F32)],
        compiler_params=_params(2), name="fox_gate")(f, b_f, tri)


def _rope_table_kernel(pos_ref, inv_ref, cos_sin_ref):
    ang = pos_ref[0].astype(F32) * inv_ref[...]
    cos_sin_ref[0] = jnp.concatenate([jnp.cos(ang), jnp.sin(ang)], axis=1)


def _rope_table(positions):
    B, S = positions.shape
    half = MLA_ROPE // 2
    inv_freq = ROPE_THETA ** (-jnp.arange(0, half, dtype=F32) * 2.0 / MLA_ROPE)
    inv2 = jnp.concatenate([inv_freq, inv_freq])[None, :]
    return pl.pallas_call(
        _rope_table_kernel, grid=(B,),
        in_specs=[pl.BlockSpec((1, S, 1), lambda b: (b, 0, 0)),
                  pl.BlockSpec((1, MLA_ROPE), lambda b: (0, 0))],
        out_specs=pl.BlockSpec((1, S, 2 * MLA_ROPE), lambda b: (b, 0, 0)),
        out_shape=jax.ShapeDtypeStruct((B, S, 2 * MLA_ROPE), F32),
        compiler_params=_params(1), name="rope_table")(positions.reshape(B, S, 1), inv2)


def _mla_down_kernel(x_ref, g_ref, w_ref, qn_ref, kvn_ref, cos_sin_ref, cq_ref, ckv_ref, kr_ref):
    kv0 = MLA_Q_RANK
    r0 = MLA_Q_RANK + MLA_KV_RANK
    sub = x_ref.shape[1] // OUT_ROW_SPLIT

    def product(r):
        h = (_rms(x_ref[0, r * sub:(r + 1) * sub, :]) * g_ref[...]).astype(BF16)
        return lax.dot_general(h, w_ref[...], _NT, preferred_element_type=F32)

    pending = [product(0)]
    for r in range(OUT_ROW_SPLIT):
        if r + 1 < OUT_ROW_SPLIT:
            pending.append(product(r + 1))
        rows = slice(r * sub, (r + 1) * sub)
        down = pending.pop(0)
        cq_ref[0, rows, :] = (_rms(down[:, :kv0]) * qn_ref[...]).astype(BF16)
        ckv_ref[0, rows, :] = (_rms(down[:, kv0:r0]) * kvn_ref[...]).astype(BF16)
        kr_ref[0, rows, :] = _rope_pair(down[:, r0:r0 + 2 * MLA_ROPE], cos_sin_ref[0, rows, :])


def _mla_down(x, gain, w_down_t, q_norm, kv_norm, cos_sin, tm=512):
    B, S, K = x.shape
    n_down = w_down_t.shape[0]
    row = lambda b, i: (b, i, 0)
    fixed = lambda b, i: (0, 0)
    return pl.pallas_call(
        _mla_down_kernel, grid=(B, S // tm),
        in_specs=[pl.BlockSpec((1, tm, K), row), pl.BlockSpec((1, K), fixed),
                  pl.BlockSpec((n_down, K), fixed),
                  pl.BlockSpec((1, MLA_Q_RANK), fixed), pl.BlockSpec((1, MLA_KV_RANK), fixed),
                  pl.BlockSpec((1, tm, 2 * MLA_ROPE), row)],
        out_specs=[pl.BlockSpec((1, tm, MLA_Q_RANK), row), pl.BlockSpec((1, tm, MLA_KV_RANK), row),
                   pl.BlockSpec((1, tm, 2 * MLA_ROPE), row)],
        out_shape=[jax.ShapeDtypeStruct((B, S, MLA_Q_RANK), BF16),
                   jax.ShapeDtypeStruct((B, S, MLA_KV_RANK), BF16),
                   jax.ShapeDtypeStruct((B, S, 2 * MLA_ROPE), F32)],
        compiler_params=_params(2), name="mla_down")(x, gain, w_down_t, q_norm, kv_norm, cos_sin)


def _out_proj(o, w_out, x, mlp_gain, *, name):
    return _matmul(o, w_out[None], res=x, next_gain=mlp_gain, out_dtype=F32, tm=OUT_TM,
                   tn=w_out.shape[1], row_split=OUT_ROW_SPLIT, name=name)


def _sb_layer(x, gain, mlp_gain, w_in, w_out, j, w2_cast):
    B, S, _ = x.shape
    qkv = _matmul(x, w_in, j, gain=gain, out_dtype=BF16, name="sb_qkv")
    o, w_out_bf16, w2_bf16 = _sb_attention(qkv, B, S, [(w_out, j), w2_cast])
    return (*_out_proj(o, w_out_bf16, x, mlp_gain, name="sb_out"), w2_bf16)


def _fox_layer(x, gain, mlp_gain, w_in, b_f, q_gain, k_gain, w_out, j, w2_cast):
    B, S, _ = x.shape
    w_f = jnp.pad(w_in[j, :, 3 * ATTN_WIDTH:], ((0, 0), (0, LANES - N_HEADS))).astype(BF16)
    b_pad = jnp.pad(b_f, (0, LANES - N_HEADS))[None, :]
    qkv_gain = jnp.concatenate([jnp.tile(q_gain, N_HEADS), jnp.tile(k_gain, N_HEADS),
                                jnp.ones((ATTN_WIDTH,), F32)])[None, :]
    qkv, f = _matmul(x, jnp.swapaxes(w_in, 1, 2), j, w_rows=True, n_cols=3 * ATTN_WIDTH, gain=gain,
                     epi=_head_norm_epi, epi_tiles=2 * ATTN_WIDTH // MM_TN, extras=(qkv_gain,),
                     side_w=w_f, out_dtype=BF16, row_split=OUT_ROW_SPLIT, name="fox_qkv")
    cf_col, cf_rowT = _fox_gate(f, b_pad)
    cf_row = cf_rowT[:, :N_HEADS, :].reshape(B, N_HEADS, 1, S)
    o, w_out_bf16, w2_bf16 = _fox_attention(qkv, cf_col, cf_row, B, S, [(w_out, j), w2_cast])
    return (*_out_proj(o, w_out_bf16, x, mlp_gain, name="fox_out"), w2_bf16)


def _mla_layer(x, positions, gain, mlp_gain, w_in, q_norm, kv_norm, w_uq, w_ukv, q_gain, k_gain,
               w_out_all, j, w2_cast):
    B, S, _ = x.shape
    r0 = MLA_Q_RANK + MLA_KV_RANK
    w_down = jnp.concatenate([w_in.T, _rot_columns(w_in[:, r0:]).T], axis=0).astype(BF16)
    w_q = w_uq.reshape(MLA_Q_RANK, N_HEADS, MLA_QK)
    w_q = jnp.concatenate([w_q, _rot_columns(w_q[:, :, MLA_NOPE:])], axis=2)
    w_q = w_q.reshape(MLA_Q_RANK, -1).astype(BF16)
    w_kv = w_ukv.reshape(MLA_KV_RANK, N_HEADS, MLA_NOPE + MLA_V)
    w_k = w_kv[:, :, :MLA_NOPE].reshape(MLA_KV_RANK, -1).astype(BF16)
    w_v = w_kv[:, :, MLA_NOPE:].reshape(MLA_KV_RANK, -1).astype(BF16)
    qg = jnp.tile(jnp.pad(q_gain, (0, MLA_QK_PAD - MLA_QK)), N_HEADS)[None, :]
    kg = jnp.tile(jnp.pad(k_gain, (0, MLA_QK_PAD - MLA_QK)), N_HEADS)[None, :]

    cos_sin = _rope_table(positions)
    c_q, c_kv, k_rope = _mla_down(x, gain, w_down, q_norm[None, :], kv_norm[None, :], cos_sin)
    q = _matmul(c_q, w_q[None], epi=_mla_q_epi, extras=(qg,), row_extras=(cos_sin,), out_dtype=BF16,
                row_split=EPI_ROW_SPLIT, name="mla_uq")
    k = _matmul(c_kv, w_k[None], epi=_mla_k_epi, extras=(kg,), row_extras=(k_rope,), out_dtype=BF16,
                out_tn=MM_TN * MLA_QK_PAD // MLA_NOPE, row_split=EPI_ROW_SPLIT, name="mla_uk")
    v = _matmul(c_kv, w_v[None], out_dtype=BF16, name="mla_uv")
    o, w_out_bf16, w2_bf16 = _mla_attention(q, k, v, B, S, [(w_out_all, j), w2_cast])
    return (*_out_proj(o, w_out_bf16, x, mlp_gain, name="mla_out"), w2_bf16)


def _mlp(x, h, w1, w2_bf16, i):
    a = _matmul(h, w1, i, epi=_relu2_epi, out_dtype=BF16, tm=MLP_UP_TM, name="mlp_up")
    return _matmul_ksplit(a, w2_bf16[None], 0, x, name="mlp_down")


def kernel(x, positions, mix_norm, mlp_norm, sb_w_in, sb_w_out, fox_w_in, fox_b_f, fox_q_gain,
           fox_k_gain, fox_w_out, mla_w_in, mla_q_norm, mla_kv_norm, mla_w_uq, mla_w_ukv,
           mla_q_gain, mla_k_gain, mla_w_out, mlp_w1, mlp_w2):
    depth = mix_norm.shape[0]
    for i in range(depth):
        kind, j = i % N_MIXERS, i // N_MIXERS
        gain = mix_norm[i][None, :]
        mlp_gain = mlp_norm[i][None, :]
        w2_cast = (mlp_w2, i)
        if kind == 0:
            x, h, w2_bf16 = _sb_layer(x, gain, mlp_gain, sb_w_in, sb_w_out, j, w2_cast)
        elif kind == 1:
            x, h, w2_bf16 = _fox_layer(x, gain, mlp_gain, fox_w_in, fox_b_f[j], fox_q_gain[j],
                                       fox_k_gain[j], fox_w_out, j, w2_cast)
        else:
            x, h, w2_bf16 = _mla_layer(x, positions, gain, mlp_gain, mla_w_in[j], mla_q_norm[j],
                                       mla_kv_norm[j], mla_w_uq[j], mla_w_ukv[j], mla_q_gain[j],
                                       mla_k_gain[j], mla_w_out, j, w2_cast)
        x = _mlp(x, h, mlp_w1, w2_bf16, i)
    return x
```

```python
import functools
import math

import jax
import jax.numpy as jnp
from jax import lax
from jax.experimental import pallas as pl
from jax.experimental.pallas import tpu as pltpu

F32 = jnp.float32
BF16 = jnp.bfloat16

D_MODEL = 2048
HEAD_DIM = 128
N_HEADS = D_MODEL // HEAD_DIM
ATTN_WIDTH = N_HEADS * HEAD_DIM
D_FF = 4 * D_MODEL
MLA_Q_RANK = 3 * D_MODEL // 8
MLA_KV_RANK = D_MODEL // 4
MLA_NOPE = 128
MLA_ROPE = 64
MLA_V = 128
MLA_QK = MLA_NOPE + MLA_ROPE
MLA_QK_PAD = 256
ROPE_THETA = 10000.0
EPS = 1e-6
N_MIXERS = 3
LOG2E = math.log2(math.e)

LANES = 128
ATTN_TQ = 256
ATTN_TK = 256
SCORE_LOOKAHEAD = 5
ONES_ROWS = 16
PREP_AHEAD = 4
SCAN_LOOKAHEAD = 2
MM_TM = 1024
MM_TN = 1024
MLP_UP_TM = 2048
MLP_TK = 4096
OUT_TM = 512
OUT_ROW_SPLIT = 2
EPI_ROW_SPLIT = 4
MM_VMEM_LIMIT = 56 * 1024 * 1024
ATTN_VMEM_LIMIT = 48 * 1024 * 1024

_NT = (((1,), (1,)), ((), ()))


def _params(n_grid, vmem_limit=None):
    return pltpu.CompilerParams(dimension_semantics=("arbitrary",) * n_grid,
                                vmem_limit_bytes=MM_VMEM_LIMIT if vmem_limit is None else vmem_limit)


def _rms(x):
    ms = jnp.mean(x * x, axis=-1, keepdims=True)
    return x * lax.rsqrt(ms + EPS)


def _rope_pair(pair, cos_sin):
    prod = pair * cos_sin
    return prod + pltpu.roll(prod, MLA_ROPE, axis=1)


def _rot_columns(w_rope):
    half = w_rope.shape[-1] // 2
    return jnp.concatenate([-w_rope[..., half:], w_rope[..., :half]], axis=-1)


def _mm_kernel(*refs, norm, w_rows, w_outer, epi, epi_tiles, n_extra, n_row_extra, has_res, emit_norm,
               side, row_split):
    it = iter(refs)
    x_ref = next(it)
    g_ref = next(it) if norm else None
    w_ref = next(it)
    extra = [next(it) for _ in range(n_extra)]
    row_extra = [next(it) for _ in range(n_row_extra)]
    res_ref = next(it) if has_res else None
    gn_ref = next(it) if emit_norm else None
    sw_ref = next(it) if side else None
    o_ref = next(it)
    hn_ref = next(it) if emit_norm else None
    so_ref = next(it) if side else None
    h_ref = next(it) if norm else None

    tm = o_ref.shape[1]
    j = pl.program_id(1 if w_outer else 2)
    row0 = pl.multiple_of(pl.program_id(2) * tm, tm) if w_outer else 0
    if norm:
        @pl.when(j == 0)
        def _():
            h = (_rms(x_ref[0]) * g_ref[...]).astype(BF16)
            h_ref[pl.ds(row0, tm), :] = h
            if side:
                so_ref[0] = jnp.dot(h, sw_ref[...], preferred_element_type=F32)

    w = w_ref[0].astype(BF16)
    sub = tm // row_split
    active = None if epi_tiles is None else j < epi_tiles

    def product(r):
        rows = slice(r * sub, (r + 1) * sub)
        h = h_ref[pl.ds(row0 + r * sub, sub), :] if norm else x_ref[0, rows, :]
        if w_rows:
            return lax.dot_general(h, w, _NT, preferred_element_type=F32)
        return jnp.dot(h, w, preferred_element_type=F32)

    pending = [product(0)]
    for r in range(row_split):
        if r + 1 < row_split:
            pending.append(product(r + 1))
        rows = slice(r * sub, (r + 1) * sub)
        acc = pending.pop(0)
        if has_res:
            acc = res_ref[0, rows, :] + acc
        if epi is not None:
            acc = epi(acc, *extra, *[e[0, rows, :] for e in row_extra], active=active)
        o_ref[0, rows, :] = acc.astype(o_ref.dtype)
        if emit_norm:
            hn_ref[0, rows, :] = (_rms(acc) * gn_ref[...]).astype(BF16)


def _matmul(x, w, layer=0, *, w_rows=False, n_cols=None, gain=None, epi=None, epi_tiles=None, extras=(),
            row_extras=(), res=None, next_gain=None, side_w=None, out_dtype, tm=MM_TM, tn=MM_TN,
            out_tn=None, row_split=1, w_outer=False, name):
    B, S, K = x.shape
    N = w.shape[1 if w_rows else 2] if n_cols is None else n_cols
    out_tn = tn if out_tn is None else out_tn
    n_out = N // tn * out_tn
    norm = gain is not None
    emit_norm = next_gain is not None
    assert not emit_norm or tn == N
    n_i, n_j = S // tm, N // tn
    if w_outer:
        assert norm and not emit_norm and res is None
        grid = (B, n_j, n_i)
        ix = lambda f: (lambda b, j, i: f(b, i, j))
        first_pass = lambda b, i, j: (b, jnp.where(j == 0, i, n_i - 1), 0)
    else:
        grid = (B, n_i, n_j)
        ix = lambda f: f
        first_pass = lambda b, i, j: (b, i, 0)
    tile = ix(lambda b, i, j: (b, i, j))
    fixed = ix(lambda b, i, j: (0, 0))
    in_specs = [pl.BlockSpec((1, tm, K), ix(first_pass if norm else (lambda b, i, j: (b, i, 0))))]
    args = [x]
    if norm:
        in_specs.append(pl.BlockSpec((1, K), fixed))
        args.append(gain)
    if w_rows:
        in_specs.append(pl.BlockSpec((1, tn, K), ix(lambda b, i, j: (layer, j, 0))))
    else:
        in_specs.append(pl.BlockSpec((1, K, tn), ix(lambda b, i, j: (layer, 0, j))))
    args.append(w)
    for e in extras:
        in_specs.append(pl.BlockSpec((1, out_tn), ix(lambda b, i, j: (0, j))))
        args.append(e)
    for e in row_extras:
        in_specs.append(pl.BlockSpec((1, tm, e.shape[2]), ix(lambda b, i, j: (b, i, 0))))
        args.append(e)
    if res is not None:
        in_specs.append(pl.BlockSpec((1, tm, tn), tile))
        args.append(res)
    out_specs = pl.BlockSpec((1, tm, out_tn), tile)
    out_shape = jax.ShapeDtypeStruct((B, S, n_out), out_dtype)
    if emit_norm:
        in_specs.append(pl.BlockSpec((1, N), fixed))
        args.append(next_gain)
        out_specs = [out_specs, pl.BlockSpec((1, tm, tn), tile)]
        out_shape = [out_shape, jax.ShapeDtypeStruct((B, S, N), BF16)]
    side = side_w is not None
    if side:
        assert norm and not emit_norm
        width = side_w.shape[1]
        in_specs.append(pl.BlockSpec((K, width), fixed))
        args.append(side_w)
        out_specs = [out_specs, pl.BlockSpec((1, tm, width), ix(first_pass))]
        out_shape = [out_shape, jax.ShapeDtypeStruct((B, S, width), F32)]
    body = functools.partial(_mm_kernel, norm=norm, w_rows=w_rows, w_outer=w_outer, epi=epi,
                             epi_tiles=epi_tiles, n_extra=len(extras), n_row_extra=len(row_extras),
                             has_res=res is not None, emit_norm=emit_norm, side=side,
                             row_split=row_split)
    return pl.pallas_call(
        body, grid=grid, in_specs=in_specs, out_specs=out_specs, out_shape=out_shape,
        scratch_shapes=[pltpu.VMEM((S if w_outer else tm, K), BF16)] if norm else [],
        compiler_params=_params(3), name=name)(*args)


def _mm_ksplit_kernel(x_ref, w_ref, res_ref, o_ref):
    @pl.when(pl.program_id(3) == 0)
    def _():
        o_ref[0] = res_ref[0]

    o_ref[0] += jnp.dot(x_ref[0], w_ref[0].astype(BF16), preferred_element_type=F32)


def _matmul_ksplit(x, w, layer, res, *, tm=MM_TM, tn=MM_TN, tk=MLP_TK, name):
    B, S, K = x.shape
    N = w.shape[2]
    return pl.pallas_call(
        _mm_ksplit_kernel, grid=(B, S // tm, N // tn, K // tk),
        in_specs=[pl.BlockSpec((1, tm, tk), lambda b, i, j, k: (b, i, k)),
                  pl.BlockSpec((1, tk, tn), lambda b, i, j, k: (layer, k, j)),
                  pl.BlockSpec((1, tm, tn), lambda b, i, j, k: (b, i, j))],
        out_specs=pl.BlockSpec((1, tm, tn), lambda b, i, j, k: (b, i, j)),
        out_shape=jax.ShapeDtypeStruct((B, S, N), F32),
        compiler_params=_params(4), name=name)(x, w, res)


def _relu2_epi(acc, active=None):
    a = jnp.maximum(acc, 0.0)
    return a * a


def _head_norm_epi(acc, g_ref, active=None):
    g = g_ref[...]
    outs = []
    for c in range(acc.shape[1] // HEAD_DIM):
        sl = slice(c * HEAD_DIM, (c + 1) * HEAD_DIM)
        blk = acc[:, sl]
        inv = lax.rsqrt(jnp.mean(blk * blk, axis=-1, keepdims=True) + EPS)
        if active is not None:
            inv = jnp.where(active, inv, 1.0)
        outs.append(blk * inv * g[:, sl])
    return jnp.concatenate(outs, axis=1)


def _mla_head_norm(nope, roped_twice, g):
    sq = nope * nope + 0.5 * (roped_twice * roped_twice)
    inv = lax.rsqrt(jnp.sum(sq, axis=-1, keepdims=True) * (1.0 / MLA_QK) + EPS)
    return jnp.concatenate([nope, roped_twice], axis=1) * inv * g


def _mla_q_epi(acc, g_ref, cos_sin, active=None):
    g = g_ref[...]
    outs = []
    for c in range(acc.shape[1] // MLA_QK_PAD):
        head = acc[:, c * MLA_QK_PAD:(c + 1) * MLA_QK_PAD]
        roped = _rope_pair(head[:, MLA_NOPE:], cos_sin)
        outs.append(_mla_head_norm(head[:, :MLA_NOPE], roped, g[:, c * MLA_QK_PAD:(c + 1) * MLA_QK_PAD]))
    return jnp.concatenate(outs, axis=1)


def _mla_k_epi(acc, g_ref, k_rope, active=None):
    g = g_ref[...]
    outs = []
    for c in range(acc.shape[1] // MLA_NOPE):
        outs.append(_mla_head_norm(acc[:, c * MLA_NOPE:(c + 1) * MLA_NOPE], k_rope,
                                   g[:, c * MLA_QK_PAD:(c + 1) * MLA_QK_PAD]))
    return jnp.concatenate(outs, axis=1)


def _causal_tiles(S):
    return [(qi, list(range(qi + 1))) for qi in range(S // ATTN_TQ)]


def _tile_masks():
    key = lax.broadcasted_iota(jnp.int32, (ATTN_TK, ATTN_TQ), 0)
    qry = lax.broadcasted_iota(jnp.int32, (ATTN_TK, ATTN_TQ), 1)
    return key, qry


def _raw_scores(k_ref, q_ref, qi, ki):
    return lax.dot_general(k_ref[ki * ATTN_TK:(ki + 1) * ATTN_TK, :],
                           q_ref[qi * ATTN_TQ:(qi + 1) * ATTN_TQ, :], _NT,
                           preferred_element_type=F32)


def _transpose_v_tile(vT_ref, v_tile, ki):
    vt = v_tile(ki).T.astype(BF16)
    vT_ref[0:vt.shape[0], ki * ATTN_TK:(ki + 1) * ATTN_TK] = vt


class _score_issuer:
    def __init__(self, k_ref, q_ref, prep, n_blocks):
        self.k_ref, self.q_ref, self.prep, self.n_blocks = k_ref, q_ref, prep, n_blocks
        self.ready = 0
        for _ in range(min(PREP_AHEAD, n_blocks)):
            self._prepare_next()

    def _prepare_next(self):
        self.prep(self.ready)
        self.ready += 1

    def prepare_ahead(self, qi):
        if qi + PREP_AHEAD < self.n_blocks:
            self._prepare_next()

    def __call__(self, tile):
        qi, ki = tile
        assert max(qi, ki) < self.ready, "score tile issued before its rows were prepared"
        return _raw_scores(self.k_ref, self.q_ref, qi, ki)


def _softmax_sweep(q_ref, k_ref, vT_ref, o_ref, *, prep, scale, bias, S):
    tq, tk = ATTN_TQ, ATTN_TK
    c = scale * LOG2E
    key, qry = _tile_masks()
    causal = key <= qry
    blocks = _causal_tiles(S)
    order = [(qi, ki) for qi, kis in blocks for ki in kis]
    dv = vT_ref.shape[0] - ONES_ROWS
    vT_ref[dv:, :] = jnp.ones((ONES_ROWS, S), BF16)
    issue = _score_issuer(k_ref, q_ref, prep, len(blocks))
    pending = [issue(t) for t in order[:SCORE_LOOKAHEAD]]
    issued = SCORE_LOOKAHEAD
    for qi, kis in blocks:
        issue.prepare_ahead(qi)
        m = acc = None
        for ki in kis:
            s_raw = pending.pop(0)
            if issued < len(order):
                pending.append(issue(order[issued]))
                issued += 1
            s = s_raw * c
            b = bias(ki, qi)
            if b is not None:
                s = s + b
            if ki == qi:
                s = jnp.where(causal, s, -jnp.inf)
            s_max = jnp.max(s, axis=0, keepdims=True)
            m_new = s_max if m is None else jnp.maximum(m, s_max)
            p = jnp.exp2(s - m_new)
            pv = jnp.dot(vT_ref[:, ki * tk:(ki + 1) * tk], p.astype(BF16),
                         preferred_element_type=F32)
            acc = pv if m is None else jnp.exp2(m - m_new) * acc + pv
            m = m_new
        o = acc[:dv] / acc[dv:dv + 1]
        o_ref[0, qi * tq:(qi + 1) * tq, :] = o.T.astype(o_ref.dtype)


def _sb_attn_kernel(q_ref, k_ref, v_ref, u_ref, o_ref, vT_ref, *, scale, S):
    tq, tk = ATTN_TQ, ATTN_TK
    c = scale * LOG2E
    key, qry = _tile_masks()
    strict = key < qry
    blocks = _causal_tiles(S)
    order = [(qi, ki) for qi, kis in blocks for ki in reversed(kis)]
    q2, k2 = q_ref.at[0], k_ref.at[0]

    def v_tile(ki):
        return v_ref[0, ki * tk:(ki + 1) * tk, :].astype(F32)

    def gate(t, z_raw):
        qi, ki = t
        z = z_raw * c
        soft = jnp.log2(1.0 + jnp.exp2(-jnp.abs(z)))
        nlk = jnp.maximum(z, 0.0) + soft
        if ki == qi:
            nlk = jnp.where(strict, nlk, 0.0)
        hi = nlk.astype(BF16)
        lo = (nlk - hi.astype(F32)).astype(BF16)
        scan = jnp.dot(u_ref[...], jnp.concatenate([hi, lo], axis=0),
                       preferred_element_type=F32)
        return z, scan

    issue = _score_issuer(k2, q2, functools.partial(_transpose_v_tile, vT_ref, v_tile), len(blocks))
    scored = [issue(t) for t in order[:SCORE_LOOKAHEAD]]
    gated = []
    n_scored = [SCORE_LOOKAHEAD]
    n_gated = [0]

    def advance():
        if n_scored[0] < len(order):
            scored.append(issue(order[n_scored[0]]))
            n_scored[0] += 1
        if n_gated[0] < len(order):
            gated.append(gate(order[n_gated[0]], scored.pop(0)))
            n_gated[0] += 1

    for _ in range(SCAN_LOOKAHEAD):
        advance()
    for qi, kis in blocks:
        issue.prepare_ahead(qi)
        suf = acc = None
        for ki in reversed(kis):
            z, scan = gated.pop(0)
            advance()
            x = z + scan
            if suf is not None:
                x = x + suf
            a = jnp.exp2(x)
            if ki == qi:
                a = jnp.where(strict, a, 0.0)
            pv = jnp.dot(vT_ref[:, ki * tk:(ki + 1) * tk], a.astype(BF16),
                         preferred_element_type=F32)
            acc = pv if acc is None else acc + pv
            col = scan[0:1, :]
            suf = col if suf is None else suf + col
        o_ref[0, qi * tq:(qi + 1) * tq, :] = acc.T.astype(o_ref.dtype)


def _attention_call(body, args, in_specs, out_spec, out_shape, scratch_shapes, side_casts, B, name):
    n_in, n_side = len(args), len(side_casts)
    steps = B * N_HEADS
    in_specs, out_specs, out_shapes = list(in_specs), [out_spec], [out_shape]
    for w, layer in side_casts:
        _, R, C = w.shape
        rows = R // steps
        in_specs.append(pl.BlockSpec((1, rows, C), lambda b, h, layer=layer: (layer, b * N_HEADS + h, 0)))
        out_specs.append(pl.BlockSpec((rows, C), lambda b, h: (b * N_HEADS + h, 0)))
        out_shapes.append(jax.ShapeDtypeStruct((R, C), BF16))

    def kernel_body(*refs):
        ins, side_in = refs[:n_in], refs[n_in:n_in + n_side]
        o_ref, side_out = refs[n_in + n_side], refs[n_in + n_side + 1:n_in + 2 * n_side + 1]
        for src, dst in zip(side_in, side_out):
            dst[...] = src[0].astype(BF16)
        body(*ins, o_ref, *refs[n_in + 2 * n_side + 1:])

    return pl.pallas_call(
        kernel_body, grid=(B, N_HEADS), in_specs=in_specs, out_specs=out_specs, out_shape=out_shapes,
        scratch_shapes=scratch_shapes, compiler_params=_params(2, ATTN_VMEM_LIMIT), name=name)(
            *args, *[w for w, _ in side_casts])


def _sb_attention(qkv, B, S, side_casts):
    tk = ATTN_TK
    upper = (jnp.arange(tk)[None, :] >= jnp.arange(tk)[:, None])
    neg_u2 = jnp.tile(jnp.where(upper, -1.0, 0.0), (1, 2)).astype(BF16)
    H = N_HEADS
    return _attention_call(
        functools.partial(_sb_attn_kernel, scale=1.0 / math.sqrt(HEAD_DIM), S=S),
        (qkv, qkv, qkv, neg_u2),
        [pl.BlockSpec((1, S, HEAD_DIM), lambda b, h: (b, 0, h)),
         pl.BlockSpec((1, S, HEAD_DIM), lambda b, h: (b, 0, H + h)),
         pl.BlockSpec((1, S, HEAD_DIM), lambda b, h: (b, 0, 2 * H + h)),
         pl.BlockSpec((tk, 2 * tk), lambda b, h: (0, 0))],
        pl.BlockSpec((1, S, HEAD_DIM), lambda b, h: (b, 0, h)),
        jax.ShapeDtypeStruct((B, S, ATTN_WIDTH), BF16),
        [pltpu.VMEM((HEAD_DIM, S), BF16)], side_casts, B, "sb_attention")


def _fox_attn_kernel(q_ref, k_ref, v_ref, cfc_ref, cfr_ref, o_ref, vT_ref, slab_ref, *, scale, S):
    h = pl.program_id(1)

    def v_tile(ki):
        return v_ref[0, ki * ATTN_TK:(ki + 1) * ATTN_TK, :].astype(F32)

    def prep(c):
        rows = slice(c * ATTN_TK, (c + 1) * ATTN_TK)
        blk = cfc_ref[0, rows, :]
        lane = lax.broadcasted_iota(jnp.int32, blk.shape, 1)
        col = jnp.sum(jnp.where(lane == h, blk, 0.0), axis=1, keepdims=True) * LOG2E
        slab_ref[rows, :] = jnp.broadcast_to(col, blk.shape)
        _transpose_v_tile(vT_ref, v_tile, c)

    def bias(ki, qi):
        cf_q = cfr_ref[0, 0, :, qi * ATTN_TQ:(qi + 1) * ATTN_TQ] * LOG2E
        cf_k = slab_ref[ki * ATTN_TK:(ki + 1) * ATTN_TK, :]
        return cf_q - jnp.concatenate([cf_k] * (ATTN_TQ // LANES), axis=1)

    _softmax_sweep(q_ref.at[0], k_ref.at[0], vT_ref, o_ref, prep=prep, scale=scale, bias=bias, S=S)


def _fox_attention(qkv, cf_col, cf_row, B, S, side_casts):
    H = N_HEADS
    return _attention_call(
        functools.partial(_fox_attn_kernel, scale=1.0 / math.sqrt(HEAD_DIM), S=S),
        (qkv, qkv, qkv, cf_col, cf_row),
        [pl.BlockSpec((1, S, HEAD_DIM), lambda b, h: (b, 0, h)),
         pl.BlockSpec((1, S, HEAD_DIM), lambda b, h: (b, 0, H + h)),
         pl.BlockSpec((1, S, HEAD_DIM), lambda b, h: (b, 0, 2 * H + h)),
         pl.BlockSpec((1, S, LANES), lambda b, h: (b, 0, 0)),
         pl.BlockSpec((1, 1, 1, S), lambda b, h: (b, h, 0, 0))],
        pl.BlockSpec((1, S, HEAD_DIM), lambda b, h: (b, 0, h)),
        jax.ShapeDtypeStruct((B, S, ATTN_WIDTH), BF16),
        [pltpu.VMEM((HEAD_DIM + ONES_ROWS, S), BF16), pltpu.VMEM((S, LANES), F32)],
        side_casts, B, "fox_attention")


def _mla_attn_kernel(q_ref, k_ref, v_ref, o_ref, vT_scr, *, scale, S):
    def v_tile(ki):
        return v_ref[0, ki * ATTN_TK:(ki + 1) * ATTN_TK, :].astype(F32)

    _softmax_sweep(q_ref.at[0], k_ref.at[0], vT_scr, o_ref,
                   prep=functools.partial(_transpose_v_tile, vT_scr, v_tile),
                   scale=scale, bias=lambda ki, qi: None, S=S)


def _mla_attention(q, k, v, B, S, side_casts):
    return _attention_call(
        functools.partial(_mla_attn_kernel, scale=1.0 / math.sqrt(MLA_QK), S=S),
        (q, k, v),
        [pl.BlockSpec((1, S, MLA_QK_PAD), lambda b, h: (b, 0, h)),
         pl.BlockSpec((1, S, MLA_QK_PAD), lambda b, h: (b, 0, h)),
         pl.BlockSpec((1, S, MLA_V), lambda b, h: (b, 0, h))],
        pl.BlockSpec((1, S, MLA_V), lambda b, h: (b, 0, h)),
        jax.ShapeDtypeStruct((B, S, N_HEADS * MLA_V), BF16),
        [pltpu.VMEM((MLA_V + ONES_ROWS, S), BF16)], side_casts, B, "mla_attention")


def _fox_gate_kernel(f_ref, b_ref, tri_ref, col_ref, row_ref, carry_ref):
    @pl.when(pl.program_id(1) == 0)
    def _():
        carry_ref[...] = jnp.zeros(carry_ref.shape, F32)

    f = f_ref[0] + b_ref[...]
    log_f = jnp.minimum(f, 0.0) - jnp.log1p(jnp.exp(-jnp.abs(f)))
    p0 = log_f.astype(BF16)
    r1 = log_f - p0.astype(F32)
    p1 = r1.astype(BF16)
    p2 = (r1 - p1.astype(F32)).astype(BF16)
    tri = tri_ref[...]
    cs = (jnp.dot(tri, p0, preferred_element_type=F32) + jnp.dot(tri, p1, preferred_element_type=F32)
          + jnp.dot(tri, p2, preferred_element_type=F32)) + carry_ref[...]
    col_ref[0] = cs
    row_ref[0] = cs.T
    carry_ref[...] = cs[cs.shape[0] - 1:, :]


def _fox_gate(f, b_f, ts=512):
    B, S, _ = f.shape
    tri = (jnp.arange(ts)[None, :] <= jnp.arange(ts)[:, None]).astype(BF16)
    return pl.pallas_call(
        _fox_gate_kernel, grid=(B, S // ts),
        in_specs=[pl.BlockSpec((1, ts, LANES), lambda b, i: (b, i, 0)),
                  pl.BlockSpec((1, LANES), lambda b, i: (0, 0)),
                  pl.BlockSpec((ts, ts), lambda b, i: (0, 0))],
        out_specs=[pl.BlockSpec((1, ts, LANES), lambda b, i: (b, i, 0)),
                   pl.BlockSpec((1, LANES, ts), lambda b, i: (b, 0, i))],
        out_shape=[jax.ShapeDtypeStruct((B, S, LANES), F32), jax.ShapeDtypeStruct((B, LANES, S), F32)],
        scratch_shapes=[pltpu.VMEM((1, LANES), F32)],
        compiler_params=_params(2), name="fox_gate")(f, b_f, tri)


def _rope_table_kernel(pos_ref, inv_ref, cos_sin_ref):
    ang = pos_ref[0].astype(F32) * inv_ref[...]
    cos_sin_ref[0] = jnp.concatenate([jnp.cos(ang), jnp.sin(ang)], axis=1)


def _rope_table(positions):
    B, S = positions.shape
    half = MLA_ROPE // 2
    inv_freq = ROPE_THETA ** (-jnp.arange(0, half, dtype=F32) * 2.0 / MLA_ROPE)
    inv2 = jnp.concatenate([inv_freq, inv_freq])[None, :]
    return pl.pallas_call(
        _rope_table_kernel, grid=(B,),
        in_specs=[pl.BlockSpec((1, S, 1), lambda b: (b, 0, 0)),
                  pl.BlockSpec((1, MLA_ROPE), lambda b: (0, 0))],
        out_specs=pl.BlockSpec((1, S, 2 * MLA_ROPE), lambda b: (b, 0, 0)),
        out_shape=jax.ShapeDtypeStruct((B, S, 2 * MLA_ROPE), F32),
        compiler_params=_params(1), name="rope_table")(positions.reshape(B, S, 1), inv2)


def _mla_down_kernel(x_ref, g_ref, w_ref, qn_ref, kvn_ref, cos_sin_ref, cq_ref, ckv_ref, kr_ref):
    kv0 = MLA_Q_RANK
    r0 = MLA_Q_RANK + MLA_KV_RANK
    sub = x_ref.shape[1] // OUT_ROW_SPLIT

    def product(r):
        h = (_rms(x_ref[0, r * sub:(r + 1) * sub, :]) * g_ref[...]).astype(BF16)
        return lax.dot_general(h, w_ref[...], _NT, preferred_element_type=F32)

    pending = [product(0)]
    for r in range(OUT_ROW_SPLIT):
        if r + 1 < OUT_ROW_SPLIT:
            pending.append(product(r + 1))
        rows = slice(r * sub, (r + 1) * sub)
        down = pending.pop(0)
        cq_ref[0, rows, :] = (_rms(down[:, :kv0]) * qn_ref[...]).astype(BF16)
        ckv_ref[0, rows, :] = (_rms(down[:, kv0:r0]) * kvn_ref[...]).astype(BF16)
        kr_ref[0, rows, :] = _rope_pair(down[:, r0:r0 + 2 * MLA_ROPE], cos_sin_ref[0, rows, :])


def _mla_down(x, gain, w_down_t, q_norm, kv_norm, cos_sin, tm=512):
    B, S, K = x.shape
    n_down = w_down_t.shape[0]
    row = lambda b, i: (b, i, 0)
    fixed = lambda b, i: (0, 0)
    return pl.pallas_call(
        _mla_down_kernel, grid=(B, S // tm),
        in_specs=[pl.BlockSpec((1, tm, K), row), pl.BlockSpec((1, K), fixed),
                  pl.BlockSpec((n_down, K), fixed),
                  pl.BlockSpec((1, MLA_Q_RANK), fixed), pl.BlockSpec((1, MLA_KV_RANK), fixed),
                  pl.BlockSpec((1, tm, 2 * MLA_ROPE), row)],
        out_specs=[pl.BlockSpec((1, tm, MLA_Q_RANK), row), pl.BlockSpec((1, tm, MLA_KV_RANK), row),
                   pl.BlockSpec((1, tm, 2 * MLA_ROPE), row)],
        out_shape=[jax.ShapeDtypeStruct((B, S, MLA_Q_RANK), BF16),
                   jax.ShapeDtypeStruct((B, S, MLA_KV_RANK), BF16),
                   jax.ShapeDtypeStruct((B, S, 2 * MLA_ROPE), F32)],
        compiler_params=_params(2), name="mla_down")(x, gain, w_down_t, q_norm, kv_norm, cos_sin)


def _out_proj(o, w_out, x, mlp_gain, *, name):
    return _matmul(o, w_out[None], res=x, next_gain=mlp_gain, out_dtype=F32, tm=OUT_TM,
                   tn=w_out.shape[1], row_split=OUT_ROW_SPLIT, name=name)


def _sb_layer(x, gain, mlp_gain, w_in, w_out, j, w2_cast):
    B, S, _ = x.shape
    qkv = _matmul(x, w_in, j, gain=gain, out_dtype=BF16, w_outer=True, name="sb_qkv")
    o, w_out_bf16, w2_bf16 = _sb_attention(qkv, B, S, [(w_out, j), w2_cast])
    return (*_out_proj(o, w_out_bf16, x, mlp_gain, name="sb_out"), w2_bf16)


def _fox_layer(x, gain, mlp_gain, w_in, b_f, q_gain, k_gain, w_out, j, w2_cast):
    B, S, _ = x.shape
    w_f = jnp.pad(w_in[j, :, 3 * ATTN_WIDTH:], ((0, 0), (0, LANES - N_HEADS))).astype(BF16)
    b_pad = jnp.pad(b_f, (0, LANES - N_HEADS))[None, :]
    qkv_gain = jnp.concatenate([jnp.tile(q_gain, N_HEADS), jnp.tile(k_gain, N_HEADS),
                                jnp.ones((ATTN_WIDTH,), F32)])[None, :]
    qkv, f = _matmul(x, jnp.swapaxes(w_in, 1, 2), j, w_rows=True, n_cols=3 * ATTN_WIDTH, gain=gain,
                     epi=_head_norm_epi, epi_tiles=2 * ATTN_WIDTH // MM_TN, extras=(qkv_gain,),
                     side_w=w_f, out_dtype=BF16, row_split=OUT_ROW_SPLIT, w_outer=True,
                     name="fox_qkv")
    cf_col, cf_rowT = _fox_gate(f, b_pad)
    cf_row = cf_rowT[:, :N_HEADS, :].reshape(B, N_HEADS, 1, S)
    o, w_out_bf16, w2_bf16 = _fox_attention(qkv, cf_col, cf_row, B, S, [(w_out, j), w2_cast])
    return (*_out_proj(o, w_out_bf16, x, mlp_gain, name="fox_out"), w2_bf16)


def _mla_layer(x, positions, gain, mlp_gain, w_in, q_norm, kv_norm, w_uq, w_ukv, q_gain, k_gain,
               w_out_all, j, w2_cast):
    B, S, _ = x.shape
    r0 = MLA_Q_RANK + MLA_KV_RANK
    w_down = jnp.concatenate([w_in.T, _rot_columns(w_in[:, r0:]).T], axis=0).astype(BF16)
    w_q = w_uq.reshape(MLA_Q_RANK, N_HEADS, MLA_QK)
    w_q = jnp.concatenate([w_q, _rot_columns(w_q[:, :, MLA_NOPE:])], axis=2)
    w_q = w_q.reshape(MLA_Q_RANK, -1).astype(BF16)
    w_kv = w_ukv.reshape(MLA_KV_RANK, N_HEADS, MLA_NOPE + MLA_V)
    w_k = w_kv[:, :, :MLA_NOPE].reshape(MLA_KV_RANK, -1).astype(BF16)
    w_v = w_kv[:, :, MLA_NOPE:].reshape(MLA_KV_RANK, -1).astype(BF16)
    qg = jnp.tile(jnp.pad(q_gain, (0, MLA_QK_PAD - MLA_QK)), N_HEADS)[None, :]
    kg = jnp.tile(jnp.pad(k_gain, (0, MLA_QK_PAD - MLA_QK)), N_HEADS)[None, :]

    cos_sin = _rope_table(positions)
    c_q, c_kv, k_rope = _mla_down(x, gain, w_down, q_norm[None, :], kv_norm[None, :], cos_sin)
    q = _matmul(c_q, w_q[None], epi=_mla_q_epi, extras=(qg,), row_extras=(cos_sin,), out_dtype=BF16,
                row_split=EPI_ROW_SPLIT, name="mla_uq")
    k = _matmul(c_kv, w_k[None], epi=_mla_k_epi, extras=(kg,), row_extras=(k_rope,), out_dtype=BF16,
                out_tn=MM_TN * MLA_QK_PAD // MLA_NOPE, row_split=EPI_ROW_SPLIT, name="mla_uk")
    v = _matmul(c_kv, w_v[None], out_dtype=BF16, name="mla_uv")
    o, w_out_bf16, w2_bf16 = _mla_attention(q, k, v, B, S, [(w_out_all, j), w2_cast])
    return (*_out_proj(o, w_out_bf16, x, mlp_gain, name="mla_out"), w2_bf16)


def _mlp(x, h, w1, w2_bf16, i):
    a = _matmul(h, w1, i, epi=_relu2_epi, out_dtype=BF16, tm=MLP_UP_TM, name="mlp_up")
    return _matmul_ksplit(a, w2_bf16[None], 0, x, name="mlp_down")


def kernel(x, positions, mix_norm, mlp_norm, sb_w_in, sb_w_out, fox_w_in, fox_b_f, fox_q_gain,
           fox_k_gain, fox_w_out, mla_w_in, mla_q_norm, mla_kv_norm, mla_w_uq, mla_w_ukv,
           mla_q_gain, mla_k_gain, mla_w_out, mlp_w1, mlp_w2):
    depth = mix_norm.shape[0]
    for i in range(depth):
        kind, j = i % N_MIXERS, i // N_MIXERS
        gain = mix_norm[i][None, :]
        mlp_gain = mlp_norm[i][None, :]
        w2_cast = (mlp_w2, i)
        if kind == 0:
            x, h, w2_bf16 = _sb_layer(x, gain, mlp_gain, sb_w_in, sb_w_out, j, w2_cast)
        elif kind == 1:
            x, h, w2_bf16 = _fox_layer(x, gain, mlp_gain, fox_w_in, fox_b_f[j], fox_q_gain[j],
                                       fox_k_gain[j], fox_w_out, j, w2_cast)
        else:
            x, h, w2_bf16 = _mla_layer(x, positions, gain, mlp_gain, mla_w_in[j], mla_q_norm[j],
                                       mla_kv_norm[j], mla_w_uq[j], mla_w_ukv[j], mla_q_gain[j],
                                       mla_k_gain[j], mla_w_out, j, w2_cast)
        x = _mlp(x, h, mlp_w1, w2_bf16, i)
    return x
```

```python
import functools
import math

import jax
import jax.numpy as jnp
from jax import lax
from jax.experimental import pallas as pl
from jax.experimental.pallas import tpu as pltpu

F32 = jnp.float32
BF16 = jnp.bfloat16

D_MODEL = 2048
HEAD_DIM = 128
N_HEADS = D_MODEL // HEAD_DIM
ATTN_WIDTH = N_HEADS * HEAD_DIM
D_FF = 4 * D_MODEL
MLA_Q_RANK = 3 * D_MODEL // 8
MLA_KV_RANK = D_MODEL // 4
MLA_NOPE = 128
MLA_ROPE = 64
MLA_V = 128
MLA_QK = MLA_NOPE + MLA_ROPE
MLA_QK_PAD = 256
ROPE_THETA = 10000.0
EPS = 1e-6
N_MIXERS = 3
LOG2E = math.log2(math.e)

LANES = 128
ATTN_TQ = 256
ATTN_TK = 256
SCORE_LOOKAHEAD = 5
ONES_ROWS = 16
PREP_AHEAD = 4
SCAN_LOOKAHEAD = 2
MM_TM = 1024
MM_TN = 1024
MLP_UP_TM = 2048
MLP_TK = 4096
OUT_TM = 512
OUT_ROW_SPLIT = 2
EPI_ROW_SPLIT = 4
MM_VMEM_LIMIT = 56 * 1024 * 1024
ATTN_VMEM_LIMIT = 48 * 1024 * 1024
ATTN_SCHEDULER_FLAGS = None
SB_HEADS_PER_STEP = 2

_NT = (((1,), (1,)), ((), ()))


def _params(n_grid, vmem_limit=None, flags=None):
    return pltpu.CompilerParams(dimension_semantics=("arbitrary",) * n_grid, flags=flags,
                                vmem_limit_bytes=MM_VMEM_LIMIT if vmem_limit is None else vmem_limit)


def _rms(x):
    ms = jnp.mean(x * x, axis=-1, keepdims=True)
    return x * lax.rsqrt(ms + EPS)


def _rope_pair(pair, cos_sin):
    prod = pair * cos_sin
    return prod + pltpu.roll(prod, MLA_ROPE, axis=1)


def _rot_columns(w_rope):
    half = w_rope.shape[-1] // 2
    return jnp.concatenate([-w_rope[..., half:], w_rope[..., :half]], axis=-1)


def _mm_kernel(*refs, norm, w_rows, w_outer, epi, epi_tiles, n_extra, n_row_extra, has_res, emit_norm,
               side, row_split):
    it = iter(refs)
    x_ref = next(it)
    g_ref = next(it) if norm else None
    w_ref = next(it)
    extra = [next(it) for _ in range(n_extra)]
    row_extra = [next(it) for _ in range(n_row_extra)]
    res_ref = next(it) if has_res else None
    gn_ref = next(it) if emit_norm else None
    sw_ref = next(it) if side else None
    o_ref = next(it)
    hn_ref = next(it) if emit_norm else None
    so_ref = next(it) if side else None
    h_ref = next(it) if norm else None

    tm = o_ref.shape[1]
    j = pl.program_id(1 if w_outer else 2)
    row0 = pl.multiple_of(pl.program_id(2) * tm, tm) if w_outer else 0
    if norm:
        @pl.when(j == 0)
        def _():
            h = (_rms(x_ref[0]) * g_ref[...]).astype(BF16)
            h_ref[pl.ds(row0, tm), :] = h
            if side:
                so_ref[0] = jnp.dot(h, sw_ref[...], preferred_element_type=F32)

    w = w_ref[0].astype(BF16)
    sub = tm // row_split
    active = None if epi_tiles is None else j < epi_tiles

    def product(r):
        rows = slice(r * sub, (r + 1) * sub)
        h = h_ref[pl.ds(row0 + r * sub, sub), :] if norm else x_ref[0, rows, :]
        if w_rows:
            return lax.dot_general(h, w, _NT, preferred_element_type=F32)
        return jnp.dot(h, w, preferred_element_type=F32)

    pending = [product(0)]
    for r in range(row_split):
        if r + 1 < row_split:
            pending.append(product(r + 1))
        rows = slice(r * sub, (r + 1) * sub)
        acc = pending.pop(0)
        if has_res:
            acc = res_ref[0, rows, :] + acc
        if epi is not None:
            acc = epi(acc, *extra, *[e[0, rows, :] for e in row_extra], active=active)
        o_ref[0, rows, :] = acc.astype(o_ref.dtype)
        if emit_norm:
            hn_ref[0, rows, :] = (_rms(acc) * gn_ref[...]).astype(BF16)


def _matmul(x, w, layer=0, *, w_rows=False, n_cols=None, gain=None, epi=None, epi_tiles=None, extras=(),
            row_extras=(), res=None, next_gain=None, side_w=None, out_dtype, tm=MM_TM, tn=MM_TN,
            out_tn=None, row_split=1, w_outer=False, name):
    B, S, K = x.shape
    N = w.shape[1 if w_rows else 2] if n_cols is None else n_cols
    out_tn = tn if out_tn is None else out_tn
    n_out = N // tn * out_tn
    norm = gain is not None
    emit_norm = next_gain is not None
    assert not emit_norm or tn == N
    n_i, n_j = S // tm, N // tn
    if w_outer:
        assert norm and not emit_norm and res is None
        grid = (B, n_j, n_i)
        ix = lambda f: (lambda b, j, i: f(b, i, j))
        first_pass = lambda b, i, j: (b, jnp.where(j == 0, i, n_i - 1), 0)
    else:
        grid = (B, n_i, n_j)
        ix = lambda f: f
        first_pass = lambda b, i, j: (b, i, 0)
    tile = ix(lambda b, i, j: (b, i, j))
    fixed = ix(lambda b, i, j: (0, 0))
    in_specs = [pl.BlockSpec((1, tm, K), ix(first_pass if norm else (lambda b, i, j: (b, i, 0))))]
    args = [x]
    if norm:
        in_specs.append(pl.BlockSpec((1, K), fixed))
        args.append(gain)
    if w_rows:
        in_specs.append(pl.BlockSpec((1, tn, K), ix(lambda b, i, j: (layer, j, 0))))
    else:
        in_specs.append(pl.BlockSpec((1, K, tn), ix(lambda b, i, j: (layer, 0, j))))
    args.append(w)
    for e in extras:
        in_specs.append(pl.BlockSpec((1, out_tn), ix(lambda b, i, j: (0, j))))
        args.append(e)
    for e in row_extras:
        in_specs.append(pl.BlockSpec((1, tm, e.shape[2]), ix(lambda b, i, j: (b, i, 0))))
        args.append(e)
    if res is not None:
        in_specs.append(pl.BlockSpec((1, tm, tn), tile))
        args.append(res)
    out_specs = pl.BlockSpec((1, tm, out_tn), tile)
    out_shape = jax.ShapeDtypeStruct((B, S, n_out), out_dtype)
    if emit_norm:
        in_specs.append(pl.BlockSpec((1, N), fixed))
        args.append(next_gain)
        out_specs = [out_specs, pl.BlockSpec((1, tm, tn), tile)]
        out_shape = [out_shape, jax.ShapeDtypeStruct((B, S, N), BF16)]
    side = side_w is not None
    if side:
        assert norm and not emit_norm
        width = side_w.shape[1]
        in_specs.append(pl.BlockSpec((K, width), fixed))
        args.append(side_w)
        out_specs = [out_specs, pl.BlockSpec((1, tm, width), ix(first_pass))]
        out_shape = [out_shape, jax.ShapeDtypeStruct((B, S, width), F32)]
    body = functools.partial(_mm_kernel, norm=norm, w_rows=w_rows, w_outer=w_outer, epi=epi,
                             epi_tiles=epi_tiles, n_extra=len(extras), n_row_extra=len(row_extras),
                             has_res=res is not None, emit_norm=emit_norm, side=side,
                             row_split=row_split)
    return pl.pallas_call(
        body, grid=grid, in_specs=in_specs, out_specs=out_specs, out_shape=out_shape,
        scratch_shapes=[pltpu.VMEM((S if w_outer else tm, K), BF16)] if norm else [],
        compiler_params=_params(3), name=name)(*args)


def _mm_ksplit_kernel(x_ref, w_ref, res_ref, o_ref):
    @pl.when(pl.program_id(3) == 0)
    def _():
        o_ref[0] = res_ref[0]

    o_ref[0] += jnp.dot(x_ref[0], w_ref[0].astype(BF16), preferred_element_type=F32)


def _matmul_ksplit(x, w, layer, res, *, tm=MM_TM, tn=MM_TN, tk=MLP_TK, name):
    B, S, K = x.shape
    N = w.shape[2]
    return pl.pallas_call(
        _mm_ksplit_kernel, grid=(B, S // tm, N // tn, K // tk),
        in_specs=[pl.BlockSpec((1, tm, tk), lambda b, i, j, k: (b, i, k)),
                  pl.BlockSpec((1, tk, tn), lambda b, i, j, k: (layer, k, j)),
                  pl.BlockSpec((1, tm, tn), lambda b, i, j, k: (b, i, j))],
        out_specs=pl.BlockSpec((1, tm, tn), lambda b, i, j, k: (b, i, j)),
        out_shape=jax.ShapeDtypeStruct((B, S, N), F32),
        compiler_params=_params(4), name=name)(x, w, res)


def _relu2_epi(acc, active=None):
    a = jnp.maximum(acc, 0.0)
    return a * a


def _head_norm_epi(acc, g_ref, active=None):
    g = g_ref[...]
    outs = []
    for c in range(acc.shape[1] // HEAD_DIM):
        sl = slice(c * HEAD_DIM, (c + 1) * HEAD_DIM)
        blk = acc[:, sl]
        inv = lax.rsqrt(jnp.mean(blk * blk, axis=-1, keepdims=True) + EPS)
        if active is not None:
            inv = jnp.where(active, inv, 1.0)
        outs.append(blk * inv * g[:, sl])
    return jnp.concatenate(outs, axis=1)


def _mla_head_norm(nope, roped_twice, g):
    sq = nope * nope + 0.5 * (roped_twice * roped_twice)
    inv = lax.rsqrt(jnp.sum(sq, axis=-1, keepdims=True) * (1.0 / MLA_QK) + EPS)
    return jnp.concatenate([nope, roped_twice], axis=1) * inv * g


def _mla_q_epi(acc, g_ref, cos_sin, active=None):
    g = g_ref[...]
    outs = []
    for c in range(acc.shape[1] // MLA_QK_PAD):
        head = acc[:, c * MLA_QK_PAD:(c + 1) * MLA_QK_PAD]
        roped = _rope_pair(head[:, MLA_NOPE:], cos_sin)
        outs.append(_mla_head_norm(head[:, :MLA_NOPE], roped, g[:, c * MLA_QK_PAD:(c + 1) * MLA_QK_PAD]))
    return jnp.concatenate(outs, axis=1)


def _mla_k_epi(acc, g_ref, k_rope, active=None):
    g = g_ref[...]
    outs = []
    for c in range(acc.shape[1] // MLA_NOPE):
        outs.append(_mla_head_norm(acc[:, c * MLA_NOPE:(c + 1) * MLA_NOPE], k_rope,
                                   g[:, c * MLA_QK_PAD:(c + 1) * MLA_QK_PAD]))
    return jnp.concatenate(outs, axis=1)


def _causal_tiles(S):
    return [(qi, list(range(qi + 1))) for qi in range(S // ATTN_TQ)]


def _tile_masks():
    key = lax.broadcasted_iota(jnp.int32, (ATTN_TK, ATTN_TQ), 0)
    qry = lax.broadcasted_iota(jnp.int32, (ATTN_TK, ATTN_TQ), 1)
    return key, qry


def _raw_scores(k_ref, q_ref, qi, ki):
    return lax.dot_general(k_ref[ki * ATTN_TK:(ki + 1) * ATTN_TK, :],
                           q_ref[qi * ATTN_TQ:(qi + 1) * ATTN_TQ, :], _NT,
                           preferred_element_type=F32)


def _transpose_v_tile(vT_ref, v_tile, ki):
    vt = v_tile(ki).T.astype(BF16)
    vT_ref[0:vt.shape[0], ki * ATTN_TK:(ki + 1) * ATTN_TK] = vt


class _score_issuer:
    def __init__(self, k_ref, q_ref, prep, n_blocks):
        self.k_ref, self.q_ref, self.prep, self.n_blocks = k_ref, q_ref, prep, n_blocks
        self.ready = 0
        for _ in range(min(PREP_AHEAD, n_blocks)):
            self._prepare_next()

    def _prepare_next(self):
        self.prep(self.ready)
        self.ready += 1

    def prepare_ahead(self, qi):
        if qi + PREP_AHEAD < self.n_blocks:
            self._prepare_next()

    def __call__(self, tile):
        qi, ki = tile
        assert max(qi, ki) < self.ready, "score tile issued before its rows were prepared"
        return _raw_scores(self.k_ref, self.q_ref, qi, ki)


def _softmax_sweep(q_ref, k_ref, vT_ref, o_ref, *, prep, scale, bias, S):
    tq, tk = ATTN_TQ, ATTN_TK
    c = scale * LOG2E
    key, qry = _tile_masks()
    causal = key <= qry
    blocks = _causal_tiles(S)
    order = [(qi, ki) for qi, kis in blocks for ki in kis]
    dv = vT_ref.shape[0] - ONES_ROWS
    vT_ref[dv:, :] = jnp.ones((ONES_ROWS, S), BF16)
    issue = _score_issuer(k_ref, q_ref, prep, len(blocks))
    pending = [issue(t) for t in order[:SCORE_LOOKAHEAD]]
    issued = SCORE_LOOKAHEAD
    for qi, kis in blocks:
        issue.prepare_ahead(qi)
        m = acc = None
        for ki in kis:
            s_raw = pending.pop(0)
            if issued < len(order):
                pending.append(issue(order[issued]))
                issued += 1
            s = s_raw * c
            b = bias(ki, qi)
            if b is not None:
                s = s + b
            if ki == qi:
                s = jnp.where(causal, s, -jnp.inf)
            s_max = jnp.max(s, axis=0, keepdims=True)
            m_new = s_max if m is None else jnp.maximum(m, s_max)
            p = jnp.exp2(s - m_new)
            pv = jnp.dot(vT_ref[:, ki * tk:(ki + 1) * tk], p.astype(BF16),
                         preferred_element_type=F32)
            acc = pv if m is None else jnp.exp2(m - m_new) * acc + pv
            m = m_new
        o = acc[:dv] / acc[dv:dv + 1]
        o_ref[0, qi * tq:(qi + 1) * tq, :] = o.T.astype(o_ref.dtype)


def _sb_attn_kernel(q_ref, k_ref, v_ref, u_ref, o_ref, vT_ref, *, scale, S):
    key, qry = _tile_masks()
    strict = key < qry
    sweeps = []
    for g in range(SB_HEADS_PER_STEP):
        lanes = slice(g * HEAD_DIM, (g + 1) * HEAD_DIM)
        sweeps.append(_sb_head_sweep(q_ref.at[0, :, lanes], k_ref.at[0, :, lanes], v_ref.at[0, :, lanes],
                                     u_ref, o_ref.at[0, :, lanes], vT_ref.at[g], strict,
                                     scale=scale, S=S))
    while sweeps:
        for sweep in list(sweeps):
            if next(sweep, "done") == "done":
                sweeps.remove(sweep)


def _sb_head_sweep(q2, k2, v2, u_ref, o2, vT_ref, strict, *, scale, S):
    tq, tk = ATTN_TQ, ATTN_TK
    c = scale * LOG2E
    blocks = _causal_tiles(S)
    order = [(qi, ki) for qi, kis in blocks for ki in reversed(kis)]

    def v_tile(ki):
        return v2[ki * tk:(ki + 1) * tk, :].astype(F32)

    def gate(t, z_raw):
        qi, ki = t
        z = z_raw * c
        soft = jnp.log2(1.0 + jnp.exp2(-jnp.abs(z)))
        nlk = jnp.maximum(z, 0.0) + soft
        if ki == qi:
            nlk = jnp.where(strict, nlk, 0.0)
        hi = nlk.astype(BF16)
        lo = (nlk - hi.astype(F32)).astype(BF16)
        scan = jnp.dot(u_ref[...], jnp.concatenate([hi, lo], axis=0),
                       preferred_element_type=F32)
        return z, scan

    issue = _score_issuer(k2, q2, functools.partial(_transpose_v_tile, vT_ref, v_tile), len(blocks))
    scored = [issue(t) for t in order[:SCORE_LOOKAHEAD]]
    gated = []
    n_scored = [SCORE_LOOKAHEAD]
    n_gated = [0]

    def advance():
        if n_scored[0] < len(order):
            scored.append(issue(order[n_scored[0]]))
            n_scored[0] += 1
        if n_gated[0] < len(order):
            gated.append(gate(order[n_gated[0]], scored.pop(0)))
            n_gated[0] += 1

    for _ in range(SCAN_LOOKAHEAD):
        advance()
    for qi, kis in blocks:
        issue.prepare_ahead(qi)
        suf = acc = None
        for ki in reversed(kis):
            z, scan = gated.pop(0)
            advance()
            x = z + scan
            if suf is not None:
                x = x + suf
            a = jnp.exp2(x)
            if ki == qi:
                a = jnp.where(strict, a, 0.0)
            pv = jnp.dot(vT_ref[:, ki * tk:(ki + 1) * tk], a.astype(BF16),
                         preferred_element_type=F32)
            acc = pv if acc is None else acc + pv
            col = scan[0:1, :]
            suf = col if suf is None else suf + col
            yield
        o2[qi * tq:(qi + 1) * tq, :] = acc.T.astype(o2.dtype)


def _attention_call(body, args, in_specs, out_spec, out_shape, scratch_shapes, side_casts, B, name,
                    heads_per_step=1):
    n_in, n_side = len(args), len(side_casts)
    n_groups = N_HEADS // heads_per_step
    steps = B * n_groups
    in_specs, out_specs, out_shapes = list(in_specs), [out_spec], [out_shape]
    for w, layer in side_casts:
        _, R, C = w.shape
        rows = R // steps
        in_specs.append(pl.BlockSpec((1, rows, C), lambda b, h, layer=layer: (layer, b * n_groups + h, 0)))
        out_specs.append(pl.BlockSpec((rows, C), lambda b, h: (b * n_groups + h, 0)))
        out_shapes.append(jax.ShapeDtypeStruct((R, C), BF16))

    def kernel_body(*refs):
        ins, side_in = refs[:n_in], refs[n_in:n_in + n_side]
        o_ref, side_out = refs[n_in + n_side], refs[n_in + n_side + 1:n_in + 2 * n_side + 1]
        for src, dst in zip(side_in, side_out):
            dst[...] = src[0].astype(BF16)
        body(*ins, o_ref, *refs[n_in + 2 * n_side + 1:])

    return pl.pallas_call(
        kernel_body, grid=(B, n_groups), in_specs=in_specs, out_specs=out_specs, out_shape=out_shapes,
        scratch_shapes=scratch_shapes,
        compiler_params=_params(2, ATTN_VMEM_LIMIT, ATTN_SCHEDULER_FLAGS), name=name)(
            *args, *[w for w, _ in side_casts])


def _sb_attention(qkv, B, S, side_casts):
    tk = ATTN_TK
    upper = (jnp.arange(tk)[None, :] >= jnp.arange(tk)[:, None])
    neg_u2 = jnp.tile(jnp.where(upper, -1.0, 0.0), (1, 2)).astype(BF16)
    G = SB_HEADS_PER_STEP
    n_groups = N_HEADS // G
    width = G * HEAD_DIM
    return _attention_call(
        functools.partial(_sb_attn_kernel, scale=1.0 / math.sqrt(HEAD_DIM), S=S),
        (qkv, qkv, qkv, neg_u2),
        [pl.BlockSpec((1, S, width), lambda b, h: (b, 0, h)),
         pl.BlockSpec((1, S, width), lambda b, h: (b, 0, n_groups + h)),
         pl.BlockSpec((1, S, width), lambda b, h: (b, 0, 2 * n_groups + h)),
         pl.BlockSpec((tk, 2 * tk), lambda b, h: (0, 0))],
        pl.BlockSpec((1, S, width), lambda b, h: (b, 0, h)),
        jax.ShapeDtypeStruct((B, S, ATTN_WIDTH), BF16),
        [pltpu.VMEM((G, HEAD_DIM, S), BF16)], side_casts, B, "sb_attention", heads_per_step=G)


def _fox_attn_kernel(q_ref, k_ref, v_ref, cfc_ref, cfr_ref, o_ref, vT_ref, slab_ref, *, scale, S):
    h = pl.program_id(1)

    def v_tile(ki):
        return v_ref[0, ki * ATTN_TK:(ki + 1) * ATTN_TK, :].astype(F32)

    def prep(c):
        rows = slice(c * ATTN_TK, (c + 1) * ATTN_TK)
        blk = cfc_ref[0, rows, :]
        lane = lax.broadcasted_iota(jnp.int32, blk.shape, 1)
        col = jnp.sum(jnp.where(lane == h, blk, 0.0), axis=1, keepdims=True) * LOG2E
        slab_ref[rows, :] = jnp.broadcast_to(col, blk.shape)
        _transpose_v_tile(vT_ref, v_tile, c)

    def bias(ki, qi):
        cf_q = cfr_ref[0, 0, :, qi * ATTN_TQ:(qi + 1) * ATTN_TQ] * LOG2E
        cf_k = slab_ref[ki * ATTN_TK:(ki + 1) * ATTN_TK, :]
        return cf_q - jnp.concatenate([cf_k] * (ATTN_TQ // LANES), axis=1)

    _softmax_sweep(q_ref.at[0], k_ref.at[0], vT_ref, o_ref, prep=prep, scale=scale, bias=bias, S=S)


def _fox_attention(qkv, cf_col, cf_row, B, S, side_casts):
    H = N_HEADS
    return _attention_call(
        functools.partial(_fox_attn_kernel, scale=1.0 / math.sqrt(HEAD_DIM), S=S),
        (qkv, qkv, qkv, cf_col, cf_row),
        [pl.BlockSpec((1, S, HEAD_DIM), lambda b, h: (b, 0, h)),
         pl.BlockSpec((1, S, HEAD_DIM), lambda b, h: (b, 0, H + h)),
         pl.BlockSpec((1, S, HEAD_DIM), lambda b, h: (b, 0, 2 * H + h)),
         pl.BlockSpec((1, S, LANES), lambda b, h: (b, 0, 0)),
         pl.BlockSpec((1, 1, 1, S), lambda b, h: (b, h, 0, 0))],
        pl.BlockSpec((1, S, HEAD_DIM), lambda b, h: (b, 0, h)),
        jax.ShapeDtypeStruct((B, S, ATTN_WIDTH), BF16),
        [pltpu.VMEM((HEAD_DIM + ONES_ROWS, S), BF16), pltpu.VMEM((S, LANES), F32)],
        side_casts, B, "fox_attention")


def _mla_attn_kernel(q_ref, k_ref, v_ref, o_ref, vT_scr, *, scale, S):
    def v_tile(ki):
        return v_ref[0, ki * ATTN_TK:(ki + 1) * ATTN_TK, :].astype(F32)

    _softmax_sweep(q_ref.at[0], k_ref.at[0], vT_scr, o_ref,
                   prep=functools.partial(_transpose_v_tile, vT_scr, v_tile),
                   scale=scale, bias=lambda ki, qi: None, S=S)


def _mla_attention(q, k, v, B, S, side_casts):
    return _attention_call(
        functools.partial(_mla_attn_kernel, scale=1.0 / math.sqrt(MLA_QK), S=S),
        (q, k, v),
        [pl.BlockSpec((1, S, MLA_QK_PAD), lambda b, h: (b, 0, h)),
         pl.BlockSpec((1, S, MLA_QK_PAD), lambda b, h: (b, 0, h)),
         pl.BlockSpec((1, S, MLA_V), lambda b, h: (b, 0, h))],
        pl.BlockSpec((1, S, MLA_V), lambda b, h: (b, 0, h)),
        jax.ShapeDtypeStruct((B, S, N_HEADS * MLA_V), BF16),
        [pltpu.VMEM((MLA_V + ONES_ROWS, S), BF16)], side_casts, B, "mla_attention")


def _fox_gate_kernel(f_ref, b_ref, tri_ref, col_ref, row_ref, carry_ref):
    @pl.when(pl.program_id(1) == 0)
    def _():
        carry_ref[...] = jnp.zeros(carry_ref.shape, F32)

    f = f_ref[0] + b_ref[...]
    log_f = jnp.minimum(f, 0.0) - jnp.log1p(jnp.exp(-jnp.abs(f)))
    p0 = log_f.astype(BF16)
    r1 = log_f - p0.astype(F32)
    p1 = r1.astype(BF16)
    p2 = (r1 - p1.astype(F32)).astype(BF16)
    tri = tri_ref[...]
    cs = (jnp.dot(tri, p0, preferred_element_type=F32) + jnp.dot(tri, p1, preferred_element_type=F32)
          + jnp.dot(tri, p2, preferred_element_type=F32)) + carry_ref[...]
    col_ref[0] = cs
    row_ref[0] = cs.T
    carry_ref[...] = cs[cs.shape[0] - 1:, :]


def _fox_gate(f, b_f, ts=512):
    B, S, _ = f.shape
    tri = (jnp.arange(ts)[None, :] <= jnp.arange(ts)[:, None]).astype(BF16)
    return pl.pallas_call(
        _fox_gate_kernel, grid=(B, S // ts),
        in_specs=[pl.BlockSpec((1, ts, LANES), lambda b, i: (b, i, 0)),
                  pl.BlockSpec((1, LANES), lambda b, i: (0, 0)),
                  pl.BlockSpec((ts, ts), lambda b, i: (0, 0))],
        out_specs=[pl.BlockSpec((1, ts, LANES), lambda b, i: (b, i, 0)),
                   pl.BlockSpec((1, LANES, ts), lambda b, i: (b, 0, i))],
        out_shape=[jax.ShapeDtypeStruct((B, S, LANES), F32), jax.ShapeDtypeStruct((B, LANES, S), F32)],
        scratch_shapes=[pltpu.VMEM((1, LANES), F32)],
        compiler_params=_params(2), name="fox_gate")(f, b_f, tri)


def _rope_table_kernel(pos_ref, inv_ref, cos_sin_ref):
    ang = pos_ref[0].astype(F32) * inv_ref[...]
    cos_sin_ref[0] = jnp.concatenate([jnp.cos(ang), jnp.sin(ang)], axis=1)


def _rope_table(positions):
    B, S = positions.shape
    half = MLA_ROPE // 2
    inv_freq = ROPE_THETA ** (-jnp.arange(0, half, dtype=F32) * 2.0 / MLA_ROPE)
    inv2 = jnp.concatenate([inv_freq, inv_freq])[None, :]
    return pl.pallas_call(
        _rope_table_kernel, grid=(B,),
        in_specs=[pl.BlockSpec((1, S, 1), lambda b: (b, 0, 0)),
                  pl.BlockSpec((1, MLA_ROPE), lambda b: (0, 0))],
        out_specs=pl.BlockSpec((1, S, 2 * MLA_ROPE), lambda b: (b, 0, 0)),
        out_shape=jax.ShapeDtypeStruct((B, S, 2 * MLA_ROPE), F32),
        compiler_params=_params(1), name="rope_table")(positions.reshape(B, S, 1), inv2)


def _mla_down_kernel(x_ref, g_ref, w_ref, qn_ref, kvn_ref, cos_sin_ref, cq_ref, ckv_ref, kr_ref):
    kv0 = MLA_Q_RANK
    r0 = MLA_Q_RANK + MLA_KV_RANK
    sub = x_ref.shape[1] // OUT_ROW_SPLIT

    def product(r):
        h = (_rms(x_ref[0, r * sub:(r + 1) * sub, :]) * g_ref[...]).astype(BF16)
        return lax.dot_general(h, w_ref[...], _NT, preferred_element_type=F32)

    pending = [product(0)]
    for r in range(OUT_ROW_SPLIT):
        if r + 1 < OUT_ROW_SPLIT:
            pending.append(product(r + 1))
        rows = slice(r * sub, (r + 1) * sub)
        down = pending.pop(0)
        cq_ref[0, rows, :] = (_rms(down[:, :kv0]) * qn_ref[...]).astype(BF16)
        ckv_ref[0, rows, :] = (_rms(down[:, kv0:r0]) * kvn_ref[...]).astype(BF16)
        kr_ref[0, rows, :] = _rope_pair(down[:, r0:r0 + 2 * MLA_ROPE], cos_sin_ref[0, rows, :])


def _mla_down(x, gain, w_down_t, q_norm, kv_norm, cos_sin, tm=512):
    B, S, K = x.shape
    n_down = w_down_t.shape[0]
    row = lambda b, i: (b, i, 0)
    fixed = lambda b, i: (0, 0)
    return pl.pallas_call(
        _mla_down_kernel, grid=(B, S // tm),
        in_specs=[pl.BlockSpec((1, tm, K), row), pl.BlockSpec((1, K), fixed),
                  pl.BlockSpec((n_down, K), fixed),
                  pl.BlockSpec((1, MLA_Q_RANK), fixed), pl.BlockSpec((1, MLA_KV_RANK), fixed),
                  pl.BlockSpec((1, tm, 2 * MLA_ROPE), row)],
        out_specs=[pl.BlockSpec((1, tm, MLA_Q_RANK), row), pl.BlockSpec((1, tm, MLA_KV_RANK), row),
                   pl.BlockSpec((1, tm, 2 * MLA_ROPE), row)],
        out_shape=[jax.ShapeDtypeStruct((B, S, MLA_Q_RANK), BF16),
                   jax.ShapeDtypeStruct((B, S, MLA_KV_RANK), BF16),
                   jax.ShapeDtypeStruct((B, S, 2 * MLA_ROPE), F32)],
        compiler_params=_params(2), name="mla_down")(x, gain, w_down_t, q_norm, kv_norm, cos_sin)


def _out_proj(o, w_out, x, mlp_gain, *, name):
    return _matmul(o, w_out[None], res=x, next_gain=mlp_gain, out_dtype=F32, tm=OUT_TM,
                   tn=w_out.shape[1], row_split=OUT_ROW_SPLIT, name=name)


def _sb_layer(x, gain, mlp_gain, w_in, w_out, j, w2_cast):
    B, S, _ = x.shape
    qkv = _matmul(x, w_in, j, gain=gain, out_dtype=BF16, w_outer=True, name="sb_qkv")
    o, w_out_bf16, w2_bf16 = _sb_attention(qkv, B, S, [(w_out, j), w2_cast])
    return (*_out_proj(o, w_out_bf16, x, mlp_gain, name="sb_out"), w2_bf16)


def _fox_layer(x, gain, mlp_gain, w_in, b_f, q_gain, k_gain, w_out, j, w2_cast):
    B, S, _ = x.shape
    w_f = jnp.pad(w_in[j, :, 3 * ATTN_WIDTH:], ((0, 0), (0, LANES - N_HEADS))).astype(BF16)
    b_pad = jnp.pad(b_f, (0, LANES - N_HEADS))[None, :]
    qkv_gain = jnp.concatenate([jnp.tile(q_gain, N_HEADS), jnp.tile(k_gain, N_HEADS),
                                jnp.ones((ATTN_WIDTH,), F32)])[None, :]
    qkv, f = _matmul(x, jnp.swapaxes(w_in, 1, 2), j, w_rows=True, n_cols=3 * ATTN_WIDTH, gain=gain,
                     epi=_head_norm_epi, epi_tiles=2 * ATTN_WIDTH // MM_TN, extras=(qkv_gain,),
                     side_w=w_f, out_dtype=BF16, row_split=OUT_ROW_SPLIT, w_outer=True,
                     name="fox_qkv")
    cf_col, cf_rowT = _fox_gate(f, b_pad)
    cf_row = cf_rowT[:, :N_HEADS, :].reshape(B, N_HEADS, 1, S)
    o, w_out_bf16, w2_bf16 = _fox_attention(qkv, cf_col, cf_row, B, S, [(w_out, j), w2_cast])
    return (*_out_proj(o, w_out_bf16, x, mlp_gain, name="fox_out"), w2_bf16)


def _mla_layer(x, positions, gain, mlp_gain, w_in, q_norm, kv_norm, w_uq, w_ukv, q_gain, k_gain,
               w_out_all, j, w2_cast):
    B, S, _ = x.shape
    r0 = MLA_Q_RANK + MLA_KV_RANK
    w_down = jnp.concatenate([w_in.T, _rot_columns(w_in[:, r0:]).T], axis=0).astype(BF16)
    w_q = w_uq.reshape(MLA_Q_RANK, N_HEADS, MLA_QK)
    w_q = jnp.concatenate([w_q, _rot_columns(w_q[:, :, MLA_NOPE:])], axis=2)
    w_q = w_q.reshape(MLA_Q_RANK, -1).astype(BF16)
    w_kv = w_ukv.reshape(MLA_KV_RANK, N_HEADS, MLA_NOPE + MLA_V)
    w_k = w_kv[:, :, :MLA_NOPE].reshape(MLA_KV_RANK, -1).astype(BF16)
    w_v = w_kv[:, :, MLA_NOPE:].reshape(MLA_KV_RANK, -1).astype(BF16)
    qg = jnp.tile(jnp.pad(q_gain, (0, MLA_QK_PAD - MLA_QK)), N_HEADS)[None, :]
    kg = jnp.tile(jnp.pad(k_gain, (0, MLA_QK_PAD - MLA_QK)), N_HEADS)[None, :]

    cos_sin = _rope_table(positions)
    c_q, c_kv, k_rope = _mla_down(x, gain, w_down, q_norm[None, :], kv_norm[None, :], cos_sin)
    q = _matmul(c_q, w_q[None], epi=_mla_q_epi, extras=(qg,), row_extras=(cos_sin,), out_dtype=BF16,
                row_split=EPI_ROW_SPLIT, name="mla_uq")
    k = _matmul(c_kv, w_k[None], epi=_mla_k_epi, extras=(kg,), row_extras=(k_rope,), out_dtype=BF16,
                out_tn=MM_TN * MLA_QK_PAD // MLA_NOPE, row_split=EPI_ROW_SPLIT, name="mla_uk")
    v = _matmul(c_kv, w_v[None], out_dtype=BF16, name="mla_uv")
    o, w_out_bf16, w2_bf16 = _mla_attention(q, k, v, B, S, [(w_out_all, j), w2_cast])
    return (*_out_proj(o, w_out_bf16, x, mlp_gain, name="mla_out"), w2_bf16)


def _mlp(x, h, w1, w2_bf16, i):
    a = _matmul(h, w1, i, epi=_relu2_epi, out_dtype=BF16, tm=MLP_UP_TM, name="mlp_up")
    return _matmul_ksplit(a, w2_bf16[None], 0, x, name="mlp_down")


def kernel(x, positions, mix_norm, mlp_norm, sb_w_in, sb_w_out, fox_w_in, fox_b_f, fox_q_gain,
           fox_k_gain, fox_w_out, mla_w_in, mla_q_norm, mla_kv_norm, mla_w_uq, mla_w_ukv,
           mla_q_gain, mla_k_gain, mla_w_out, mlp_w1, mlp_w2):
    depth = mix_norm.shape[0]
    for i in range(depth):
        kind, j = i % N_MIXERS, i // N_MIXERS
        gain = mix_norm[i][None, :]
        mlp_gain = mlp_norm[i][None, :]
        w2_cast = (mlp_w2, i)
        if kind == 0:
            x, h, w2_bf16 = _sb_layer(x, gain, mlp_gain, sb_w_in, sb_w_out, j, w2_cast)
        elif kind == 1:
            x, h, w2_bf16 = _fox_layer(x, gain, mlp_gain, fox_w_in, fox_b_f[j], fox_q_gain[j],
                                       fox_k_gain[j], fox_w_out, j, w2_cast)
        else:
            x, h, w2_bf16 = _mla_layer(x, positions, gain, mlp_gain, mla_w_in[j], mla_q_norm[j],
                                       mla_kv_norm[j], mla_w_uq[j], mla_w_ukv[j], mla_q_gain[j],
                                       mla_k_gain[j], mla_w_out, j, w2_cast)
        x = _mlp(x, h, mlp_w1, w2_bf16, i)
    return x
```
